```python
import jax, jax.numpy as jnp
from jax import lax
import numpy as np

D_MODEL = 1024
BATCH = 1
SEQ = 16384
DEPTH = 1

N_META = 16
BLOCK = 128
WINDOW = 128
NORM_EPS = 1e-6

SWA_HEADS = 8
SWA_KV_HEADS = 2
SWA_HEAD_DIM = 64
SWA_GROUP = SWA_HEADS // SWA_KV_HEADS
SWA_WIDTH = SWA_HEADS * SWA_HEAD_DIM
SWA_KV_WIDTH = SWA_KV_HEADS * SWA_HEAD_DIM

MLA_HEADS = 8
MLA_Q_LORA = 256
MLA_KV_LORA = 128
MLA_NOPE_DIM = 64
MLA_ROPE_DIM = 32
MLA_V_DIM = 64
MLA_QK_DIM = MLA_NOPE_DIM + MLA_ROPE_DIM
MLA_WIDTH = MLA_HEADS * MLA_V_DIM
ROPE_THETA = 10000.0

MIX_WIDTH = SWA_WIDTH + MLA_WIDTH
IN_SIZES = (SWA_WIDTH, SWA_KV_WIDTH, SWA_KV_WIDTH, MLA_Q_LORA, MLA_KV_LORA, MLA_ROPE_DIM)
IN_WIDTH = SWA_WIDTH + 2 * SWA_KV_WIDTH + MLA_Q_LORA + MLA_KV_LORA + MLA_ROPE_DIM

N_EXPERTS = 32
TOP_K = 4
D_FF = 1024
SWIGLU_LIMIT = 7.0
SWIGLU_ALPHA = 1.702
MOE_BLOCK = 128

kernel_name = "hymba_swa_mla_moe_encoder_block"


def rms_norm(x, g):
    xf = x.astype(jnp.float32)
    y = xf * lax.rsqrt(jnp.mean(xf * xf, axis=-1, keepdims=True) + NORM_EPS)
    return (y * g.astype(jnp.float32)).astype(x.dtype)


def alibi_slopes(n):
    return jnp.asarray(2.0 ** (-8.0 * np.arange(1, n + 1) / n), dtype=jnp.float32)


def apply_rope(x, pos):
    half = x.shape[-1] // 2
    inv = 1.0 / (ROPE_THETA ** (jnp.arange(half, dtype=jnp.float32) / half))
    ang = pos.astype(jnp.float32)[:, None] * inv[None, :]
    cos = jnp.cos(ang)[None, :, None, :]
    sin = jnp.sin(ang)[None, :, None, :]
    xf = x.astype(jnp.float32)
    x1, x2 = xf[..., :half], xf[..., half:]
    return jnp.concatenate([x1 * cos - x2 * sin, x1 * sin + x2 * cos], axis=-1).astype(x.dtype)


def sink_softmax(s, sink):
    m = jnp.maximum(jnp.max(s, axis=-1, keepdims=True), sink)
    p = jnp.exp(s - m)
    return p / (jnp.sum(p, axis=-1, keepdims=True) + jnp.exp(sink - m))


def _band_attend(q, k, v, dist, valid, slopes, sink):
    scale = SWA_HEAD_DIM ** -0.5
    s = jnp.einsum('bnqkgd,bnskd->bnkgqs', q, k, preferred_element_type=jnp.float32) * scale
    bias = -slopes.reshape(SWA_KV_HEADS, SWA_GROUP)[None, :, :, None, None] * dist[:, None, None].astype(jnp.float32)
    s = jnp.where(valid[:, None, None], s + bias, -jnp.inf)
    p = sink_softmax(s, sink.astype(jnp.float32).reshape(SWA_KV_HEADS, SWA_GROUP)[:, :, None, None])
    return jnp.einsum('bnkgqs,bnskd->bnqkgd', p.astype(v.dtype), v)


def windowed_gqa(q, k, v, sink):
    b, L = q.shape[0], q.shape[1]
    S = L - N_META
    nb = S // BLOCK
    slopes = alibi_slopes(SWA_HEADS)
    km, kr = k[:, :N_META], k[:, N_META:]
    vm, vr = v[:, :N_META], v[:, N_META:]

    qm = q[:, :N_META].reshape(b, 1, N_META, SWA_KV_HEADS, SWA_GROUP, SWA_HEAD_DIM)
    k_mb = jnp.concatenate([km, kr[:, :BLOCK]], axis=1)[:, None]
    v_mb = jnp.concatenate([vm, vr[:, :BLOCK]], axis=1)[:, None]
    d_r = N_META + jnp.arange(BLOCK)[None, :] - jnp.arange(N_META)[:, None]
    dist_m = jnp.concatenate([jnp.zeros((N_META, N_META), jnp.int32), d_r], axis=1)[None]
    valid_m = jnp.concatenate([jnp.ones((N_META, N_META), bool), d_r <= WINDOW], axis=1)[None]
    out_m = _band_attend(qm, k_mb, v_mb, dist_m, valid_m, slopes, sink)

    qr = q[:, N_META:].reshape(b, nb, BLOCK, SWA_KV_HEADS, SWA_GROUP, SWA_HEAD_DIM)

    def band(t_real, t_meta):
        tp = jnp.pad(t_real, ((0, 0), (BLOCK, BLOCK), (0, 0), (0, 0)))
        tp = tp.reshape(b, nb + 2, BLOCK, SWA_KV_HEADS, SWA_HEAD_DIM)
        win = jnp.concatenate([tp[:, :-2], tp[:, 1:-1], tp[:, 2:]], axis=2)
        meta = jnp.broadcast_to(t_meta[:, None], (b, nb, N_META, SWA_KV_HEADS, SWA_HEAD_DIM))
        return jnp.concatenate([meta, win], axis=2)

    qpos = jnp.arange(nb)[:, None] * BLOCK + jnp.arange(BLOCK)[None, :]
    kpos = (jnp.arange(nb)[:, None] - 1) * BLOCK + jnp.arange(3 * BLOCK)[None, :]
    d = jnp.abs(qpos[:, :, None] - kpos[:, None, :])
    in_range = (kpos >= 0) & (kpos < S)
    valid_w = (d <= WINDOW) & in_range[:, None, :]
    dist_r = jnp.concatenate([jnp.zeros((nb, BLOCK, N_META), d.dtype), d], axis=-1)
    valid_r = jnp.concatenate([jnp.ones((nb, BLOCK, N_META), bool), valid_w], axis=-1)
    out_r = _band_attend(qr, band(kr, km), band(vr, vm), dist_r, valid_r, slopes, sink)

    return jnp.concatenate([out_m.reshape(b, N_META, SWA_WIDTH), out_r.reshape(b, S, SWA_WIDTH)], axis=1)


def latent_attention(q_lat, kv_lat, k_rope, q_lat_g, w_q_up, kv_lat_g, w_kv_up, q_g, k_g):
    b, L = q_lat.shape[0], q_lat.shape[1]
    S = L - N_META
    nb = S // BLOCK
    pos = jnp.arange(L)
    q = (rms_norm(q_lat, q_lat_g) @ w_q_up).reshape(b, L, MLA_HEADS, MLA_QK_DIM)
    kv = (rms_norm(kv_lat, kv_lat_g) @ w_kv_up).reshape(b, L, MLA_HEADS, MLA_NOPE_DIM + MLA_V_DIM)
    k_nope, v = kv[..., :MLA_NOPE_DIM], kv[..., MLA_NOPE_DIM:]
    k = jnp.concatenate([k_nope, jnp.broadcast_to(k_rope[:, :, None, :], (b, L, MLA_HEADS, MLA_ROPE_DIM))], axis=-1)
    q = rms_norm(q, q_g)
    k = rms_norm(k, k_g)
    q = jnp.concatenate([q[..., :MLA_NOPE_DIM], apply_rope(q[..., MLA_NOPE_DIM:], pos)], axis=-1)
    k = jnp.concatenate([k[..., :MLA_NOPE_DIM], apply_rope(k[..., MLA_NOPE_DIM:], pos)], axis=-1)
    scale = MLA_QK_DIM ** -0.5

    def attend(qb):
        s = jnp.einsum('bqhd,bshd->bhqs', qb, k, preferred_element_type=jnp.float32) * scale
        p = jax.nn.softmax(s, axis=-1)
        return jnp.einsum('bhqs,bshd->bqhd', p.astype(v.dtype), v)

    out_m = attend(q[:, :N_META]).reshape(b, N_META, MLA_WIDTH)
    qr = q[:, N_META:].reshape(b, nb, BLOCK, MLA_HEADS, MLA_QK_DIM).transpose(1, 0, 2, 3, 4)
    out_r = lax.map(attend, qr)
    out_r = out_r.transpose(1, 0, 2, 3, 4).reshape(b, S, MLA_WIDTH)
    return jnp.concatenate([out_m, out_r], axis=1)


def clamped_swiglu(g, u):
    g = jnp.minimum(g, SWIGLU_LIMIT)
    u = jnp.clip(u, -SWIGLU_LIMIT, SWIGLU_LIMIT)
    return (u + 1.0) * (g * jax.nn.sigmoid(SWIGLU_ALPHA * g))


def moe_ffn(h, router_w, router_b, w_gate, b_gate, w_up, b_up, w_down, b_down):
    T, D = h.shape
    logits = jnp.dot(h, router_w, preferred_element_type=jnp.float32) + router_b.astype(jnp.float32)
    top_logit, top_idx = lax.top_k(logits, TOP_K)
    gate = jax.nn.softmax(top_logit, axis=-1)
    TK = T * TOP_K
    flat_e = top_idx.reshape(TK)
    flat_tok = jnp.repeat(jnp.arange(T, dtype=jnp.int32), TOP_K)
    flat_g = gate.reshape(TK)
    order = jnp.argsort(flat_e)
    se, stok, sg = flat_e[order], flat_tok[order], flat_g[order]
    counts = jnp.bincount(flat_e, length=N_EXPERTS)
    padded = (counts + MOE_BLOCK - 1) // MOE_BLOCK * MOE_BLOCK
    pad_end = jnp.cumsum(padded)
    pad_start = pad_end - padded
    start = jnp.cumsum(counts) - counts
    dest = pad_start[se] + jnp.arange(TK, dtype=jnp.int32) - start[se]
    n_blocks = (TK + N_EXPERTS * (MOE_BLOCK - 1)) // MOE_BLOCK
    P = n_blocks * MOE_BLOCK
    slot_tok = jnp.zeros((P,), jnp.int32).at[dest].set(stok)
    slot_gate = jnp.zeros((P,), jnp.float32).at[dest].set(sg)
    block_e = jnp.minimum(jnp.searchsorted(pad_end, jnp.arange(n_blocks) * MOE_BLOCK, side='right'), N_EXPERTS - 1)
    xs = h[slot_tok].reshape(n_blocks, MOE_BLOCK, D)

    def expert_block(args):
        xb, e = args
        g = xb @ w_gate[e] + b_gate[e]
        u = xb @ w_up[e] + b_up[e]
        return clamped_swiglu(g, u) @ w_down[e] + b_down[e]

    ys = lax.map(expert_block, (xs, block_e)).reshape(P, D)
    ys = (ys.astype(jnp.float32) * slot_gate[:, None]).astype(h.dtype)
    return jnp.zeros_like(h).at[slot_tok].add(ys)


def setup_inputs(seed: int = 0) -> dict:
    key = jax.random.key(seed)
    ks = jax.random.split(key, 24)
    f32 = jnp.float32

    def nrm(k, shape, scale):
        return jax.random.normal(k, shape, f32) * scale

    def gain(k, shape):
        return 1.0 + 0.05 * jax.random.normal(k, shape, f32)

    L = DEPTH
    return {
        "x": nrm(ks[0], (BATCH, SEQ, D_MODEL), 1.0),
        "meta_tokens": nrm(ks[1], (N_META, D_MODEL), 1.0),
        "attn_norm_g": gain(ks[2], (L, D_MODEL)),
        "w_in": nrm(ks[3], (L, D_MODEL, IN_WIDTH), D_MODEL ** -0.5),
        "swa_q_norm_g": gain(ks[4], (L, SWA_HEAD_DIM)),
        "swa_k_norm_g": gain(ks[5], (L, SWA_HEAD_DIM)),
        "swa_sink": nrm(ks[6], (L, SWA_HEADS), 0.5),
        "mla_q_lat_norm_g": gain(ks[7], (L, MLA_Q_LORA)),
        "w_mla_q_up": nrm(ks[8], (L, MLA_Q_LORA, MLA_HEADS * MLA_QK_DIM), MLA_Q_LORA ** -0.5),
        "mla_kv_lat_norm_g": gain(ks[9], (L, MLA_KV_LORA)),
        "w_mla_kv_up": nrm(ks[10], (L, MLA_KV_LORA, MLA_HEADS * (MLA_NOPE_DIM + MLA_V_DIM)), MLA_KV_LORA ** -0.5),
        "mla_q_norm_g": gain(ks[11], (L, MLA_QK_DIM)),
        "mla_k_norm_g": gain(ks[12], (L, MLA_QK_DIM)),
        "w_out": nrm(ks[13], (L, MIX_WIDTH, D_MODEL), MIX_WIDTH ** -0.5),
        "ffn_norm_g": gain(ks[14], (L, D_MODEL)),
        "router_w": nrm(ks[15], (L, D_MODEL, N_EXPERTS), D_MODEL ** -0.5),
        "router_b": nrm(ks[16], (L, N_EXPERTS), 0.01),
        "w_gate": nrm(ks[17], (L, N_EXPERTS, D_MODEL, D_FF), D_MODEL ** -0.5),
        "b_gate": nrm(ks[18], (L, N_EXPERTS, D_FF), 0.01),
        "w_up": nrm(ks[19], (L, N_EXPERTS, D_MODEL, D_FF), D_MODEL ** -0.5),
        "b_up": nrm(ks[20], (L, N_EXPERTS, D_FF), 0.01),
        "w_down": nrm(ks[21], (L, N_EXPERTS, D_FF, D_MODEL), D_FF ** -0.5),
        "b_down": nrm(ks[22], (L, N_EXPERTS, D_MODEL), 0.01),
    }


def reference(x, meta_tokens, attn_norm_g, w_in, swa_q_norm_g, swa_k_norm_g, swa_sink,
              mla_q_lat_norm_g, w_mla_q_up, mla_kv_lat_norm_g, w_mla_kv_up, mla_q_norm_g,
              mla_k_norm_g, w_out, ffn_norm_g, router_w, router_b, w_gate, b_gate, w_up, b_up,
              w_down, b_down):
    b, S, D = x.shape
    meta = jnp.broadcast_to(meta_tokens.astype(x.dtype)[None], (b, N_META, D))
    h = jnp.concatenate([meta, x], axis=1)
    L = h.shape[1]
    split_at = [int(i) for i in np.cumsum(IN_SIZES)[:-1]]
    for l in range(DEPTH):
        a = rms_norm(h, attn_norm_g[l])
        proj = a @ w_in[l]
        q_s, k_s, v_s, q_lat, kv_lat, k_rope = jnp.split(proj, split_at, axis=-1)
        q_s = rms_norm(q_s.reshape(b, L, SWA_HEADS, SWA_HEAD_DIM), swa_q_norm_g[l])
        k_s = rms_norm(k_s.reshape(b, L, SWA_KV_HEADS, SWA_HEAD_DIM), swa_k_norm_g[l])
        v_s = v_s.reshape(b, L, SWA_KV_HEADS, SWA_HEAD_DIM)
        y_a = windowed_gqa(q_s, k_s, v_s, swa_sink[l])
        y_b = latent_attention(q_lat, kv_lat, k_rope, mla_q_lat_norm_g[l], w_mla_q_up[l],
                               mla_kv_lat_norm_g[l], w_mla_kv_up[l], mla_q_norm_g[l], mla_k_norm_g[l])
        h = h + jnp.concatenate([y_a, y_b], axis=-1) @ w_out[l]
        m = rms_norm(h, ffn_norm_g[l]).reshape(b * L, D)
        h = h + moe_ffn(m, router_w[l], router_b[l], w_gate[l], b_gate[l], w_up[l], b_up[l],
                        w_down[l], b_down[l]).reshape(b, L, D)
    return h[:, N_META:]
```

```python
import functools

import numpy as np
import jax
import jax.numpy as jnp
from jax import lax
from jax.experimental import pallas as pl
from jax.experimental.pallas import tpu as pltpu

F32 = jnp.float32
BF16 = jnp.bfloat16

D_MODEL = 1024
N_META = 16
BLOCK = 128
WINDOW = 128
NORM_EPS = 1e-6

SWA_HEADS = 8
SWA_KV_HEADS = 2
SWA_GROUP = SWA_HEADS // SWA_KV_HEADS
SWA_HEAD_DIM = 64
SWA_WIDTH = SWA_HEADS * SWA_HEAD_DIM
SWA_KV_WIDTH = SWA_KV_HEADS * SWA_HEAD_DIM

MLA_HEADS = 8
MLA_Q_LORA = 256
MLA_KV_LORA = 128
MLA_NOPE_DIM = 64
MLA_ROPE_DIM = 32
MLA_V_DIM = 64
MLA_QK_DIM = MLA_NOPE_DIM + MLA_ROPE_DIM
MLA_WIDTH = MLA_HEADS * MLA_V_DIM
ROPE_THETA = 10000.0

N_EXPERTS = 32
TOP_K = 4
D_FF = 1024
SWIGLU_LIMIT = 7.0
SWIGLU_ALPHA = 1.702

LANES = 128
VT_ROWS = 80
NEG_BIG = -1e30
VMEM_LIMIT = 48 * 1024 * 1024

TM_PROJ = 256
TQ_MLA = 256
TK_MLA = 512
TM_POST = 256
TD_DISPATCH = 128
TE_FFN = 256
TC_COMBINE = 128

_NT = (((1,), (1,)), ((), ()))


def _rms(v, n):
    return v * lax.rsqrt(jnp.sum(v * v, axis=-1, keepdims=True) * (1.0 / n) + NORM_EPS)


def _proj_kernel(x_ref, valid_ref, cos_ref, sin_ref, g_attn_ref, w_in_ref, g_qs_ref, g_ks_ref,
                 g_qlat_ref, w_qup_ref, g_kvlat_ref, w_k_ref, w_vt_ref, ones_ref, g_qm_ref, g_km_ref,
                 qs_ref, ks_ref, vs_ref, qm_ref, km_ref, vt_ref):
    tm = x_ref.shape[0]
    a = _rms(x_ref[...], D_MODEL) * g_attn_ref[...]
    proj = jnp.dot(a.astype(BF16), w_in_ref[...], preferred_element_type=F32)

    lane = lax.broadcasted_iota(jnp.int32, (tm, LANES), 1)
    lo = lane < SWA_HEAD_DIM

    def seg_norm(v, g):
        v2 = v * v
        s_all = jnp.sum(v2, axis=-1, keepdims=True)
        s_lo = jnp.sum(jnp.where(lo, v2, 0.0), axis=-1, keepdims=True)
        ms = jnp.where(lo, s_lo, s_all - s_lo) * (1.0 / SWA_HEAD_DIM)
        return v * lax.rsqrt(ms + NORM_EPS) * g

    g_qs = g_qs_ref[...]
    for j in range(SWA_WIDTH // LANES):
        sl = slice(j * LANES, (j + 1) * LANES)
        qs_ref[:, sl] = seg_norm(proj[:, sl], g_qs).astype(BF16)
    o_k = SWA_WIDTH
    o_v = o_k + SWA_KV_WIDTH
    o_ql = o_v + SWA_KV_WIDTH
    o_kv = o_ql + MLA_Q_LORA
    o_kr = o_kv + MLA_KV_LORA
    ks_ref[...] = seg_norm(proj[:, o_k:o_v], g_ks_ref[...]).astype(BF16)
    vs_ref[...] = proj[:, o_v:o_ql].astype(BF16)

    cosv = cos_ref[...]
    sinv = sin_ref[...]
    first_half = lane < (MLA_NOPE_DIM + MLA_ROPE_DIM // 2)

    def norm_rope(v, g):
        ms = jnp.sum(v * v, axis=-1, keepdims=True) * (1.0 / MLA_QK_DIM)
        vn = v * lax.rsqrt(ms + NORM_EPS) * g
        rot = jnp.where(first_half,
                        pltpu.roll(vn, LANES - MLA_ROPE_DIM // 2, 1),
                        pltpu.roll(vn, MLA_ROPE_DIM // 2, 1))
        return vn * cosv + rot * sinv

    qln = _rms(proj[:, o_ql:o_kv], MLA_Q_LORA) * g_qlat_ref[...]
    qup = jnp.dot(qln.astype(BF16), w_qup_ref[...], preferred_element_type=F32)
    g_qm = g_qm_ref[...]
    for h in range(MLA_HEADS):
        qm_ref[h] = norm_rope(qup[:, h * LANES:(h + 1) * LANES], g_qm).astype(BF16)

    kvn = (_rms(proj[:, o_kv:o_kr], MLA_KV_LORA) * g_kvlat_ref[...]).astype(BF16)
    knope = jnp.dot(kvn, w_k_ref[...], preferred_element_type=F32)
    krope = proj[:, o_kr:o_kr + LANES]
    g_km = g_km_ref[...]
    for h in range(MLA_HEADS):
        km_ref[h] = norm_rope(knope[:, h * LANES:(h + 1) * LANES] + krope, g_km).astype(BF16)

    vt = lax.dot_general(w_vt_ref[...], kvn, _NT, preferred_element_type=F32)
    vt = vt + ones_ref[...] * valid_ref[...]
    for h in range(MLA_HEADS):
        vt_ref[h] = vt[h * VT_ROWS:(h + 1) * VT_ROWS].astype(BF16)


def _proj_call(x, valid, cos_t, sin_t, wts, tm):
    n = x.shape[0]
    w_names = ("g_attn", "w_in", "g_qs", "g_ks", "g_qlat", "w_qup", "g_kvlat", "w_k", "w_vt", "ones", "g_qm", "g_km")
    w_args = [wts[k] for k in w_names]

    def full(a):
        return pl.BlockSpec(a.shape, lambda i: (0,) * a.ndim)

    in_specs = [pl.BlockSpec((tm, D_MODEL), lambda i: (i, 0)),
                pl.BlockSpec((1, tm), lambda i: (0, i)),
                pl.BlockSpec((tm, LANES), lambda i: (i, 0)),
                pl.BlockSpec((tm, LANES), lambda i: (i, 0))] + [full(a) for a in w_args]
    out_shape = (jax.ShapeDtypeStruct((n, SWA_WIDTH), BF16),
                 jax.ShapeDtypeStruct((n, SWA_KV_WIDTH), BF16),
                 jax.ShapeDtypeStruct((n, SWA_KV_WIDTH), BF16),
                 jax.ShapeDtypeStruct((MLA_HEADS, n, LANES), BF16),
                 jax.ShapeDtypeStruct((MLA_HEADS, n, LANES), BF16),
                 jax.ShapeDtypeStruct((MLA_HEADS, VT_ROWS, n), BF16))
    out_specs = (pl.BlockSpec((tm, SWA_WIDTH), lambda i: (i, 0)),
                 pl.BlockSpec((tm, SWA_KV_WIDTH), lambda i: (i, 0)),
                 pl.BlockSpec((tm, SWA_KV_WIDTH), lambda i: (i, 0)),
                 pl.BlockSpec((MLA_HEADS, tm, LANES), lambda i: (0, i, 0)),
                 pl.BlockSpec((MLA_HEADS, tm, LANES), lambda i: (0, i, 0)),
                 pl.BlockSpec((MLA_HEADS, VT_ROWS, tm), lambda i: (0, 0, i)))
    return pl.pallas_call(
        _proj_kernel, grid=(n // tm,), in_specs=in_specs, out_specs=out_specs, out_shape=out_shape,
        compiler_params=pltpu.CompilerParams(dimension_semantics=("arbitrary",), vmem_limit_bytes=VMEM_LIMIT),
        name="proj",
    )(x, valid, cos_t, sin_t, *w_args)


def _swa_kernel(sink_ref, q_ref, kp_ref, kc_ref, kn_ref, vp_ref, vc_ref, vn_ref, km_ref, vm_ref, o_ref, *, slopes):
    b = pl.program_id(0)
    nb = pl.num_programs(0)
    nkeys = 3 * BLOCK + LANES
    kall = jnp.concatenate([kp_ref[...], kc_ref[...], kn_ref[...], km_ref[...]], axis=0)
    vall = jnp.concatenate([vp_ref[...], vc_ref[...], vn_ref[...], vm_ref[...]], axis=0)

    r = lax.broadcasted_iota(jnp.int32, (BLOCK, nkeys), 0)
    c = lax.broadcasted_iota(jnp.int32, (BLOCK, nkeys), 1)
    d = jnp.abs(BLOCK + r - c)
    lo_edge = jnp.where(b > 0, 0, BLOCK)
    hi_edge = jnp.where(b < nb - 1, 3 * BLOCK, 2 * BLOCK)
    in_win = (d <= WINDOW) & (c >= lo_edge) & (c < hi_edge)
    is_meta = (c >= 3 * BLOCK) & (c < 3 * BLOCK + N_META)
    valid = in_win | is_meta
    negdist = jnp.where(c < 3 * BLOCK, -d.astype(F32), 0.0)

    lane = lax.broadcasted_iota(jnp.int32, (BLOCK, LANES), 1)
    lo = lane < SWA_HEAD_DIM
    for j in range(SWA_GROUP):
        qg = q_ref[:, j * LANES:(j + 1) * LANES].astype(F32)
        q2 = jnp.concatenate([jnp.where(lo, qg, 0.0), jnp.where(lo, 0.0, qg)], axis=0).astype(BF16)
        s = lax.dot_general(q2, kall, _NT, preferred_element_type=F32)
        outs = []
        for half, h in ((0, j), (1, j + SWA_GROUP)):
            sh = s[half * BLOCK:(half + 1) * BLOCK]
            sh = jnp.where(valid, sh + slopes[h] * negdist, NEG_BIG)
            sink = sink_ref[h]
            m = jnp.maximum(jnp.max(sh, axis=-1, keepdims=True), sink)
            p = jnp.exp(sh - m)
            den = jnp.sum(p, axis=-1, keepdims=True) + jnp.exp(sink - m)
            o = jnp.dot(p.astype(BF16), vall, preferred_element_type=F32)
            outs.append(o / den)
        o_ref[:, j * LANES:(j + 1) * LANES] = jnp.where(lo, outs[0], outs[1]).astype(BF16)


def _swa_call(sink, qs, ks, vs, ks_meta, vs_meta):
    n = qs.shape[0]
    nb = n // BLOCK
    slopes = tuple(float(v) for v in 2.0 ** (-8.0 * np.arange(1, SWA_HEADS + 1) / SWA_HEADS))
    kv_prev = pl.BlockSpec((BLOCK, SWA_KV_WIDTH), lambda b: (jnp.maximum(b - 1, 0), 0))
    kv_cur = pl.BlockSpec((BLOCK, SWA_KV_WIDTH), lambda b: (b, 0))
    kv_next = pl.BlockSpec((BLOCK, SWA_KV_WIDTH), lambda b: (jnp.minimum(b + 1, nb - 1), 0))
    meta = pl.BlockSpec((LANES, SWA_KV_WIDTH), lambda b: (0, 0))
    return pl.pallas_call(
        functools.partial(_swa_kernel, slopes=slopes),
        grid=(nb,),
        in_specs=[pl.BlockSpec(memory_space=pltpu.SMEM),
                  pl.BlockSpec((BLOCK, SWA_WIDTH), lambda b: (b, 0)),
                  kv_prev, kv_cur, kv_next, kv_prev, kv_cur, kv_next, meta, meta],
        out_specs=pl.BlockSpec((BLOCK, SWA_WIDTH), lambda b: (b, 0)),
        out_shape=jax.ShapeDtypeStruct((n, SWA_WIDTH), BF16),
        compiler_params=pltpu.CompilerParams(dimension_semantics=("arbitrary",), vmem_limit_bytes=VMEM_LIMIT),
        name="swa",
    )(sink, qs, ks, ks, ks, vs, vs, vs, ks_meta, vs_meta)


def _mla_kernel(q_ref, k_ref, vt_ref, km_ref, vtm_ref, o_ref, *, tk, nk):
    q = q_ref[0]
    tq = q.shape[0]

    def tile(k, vt, m, acc):
        s = lax.dot_general(k, q, _NT, preferred_element_type=F32)
        m_new = jnp.maximum(m, jnp.max(s, axis=0, keepdims=True))
        alpha = jnp.exp(m - m_new)
        p = jnp.exp(s - m_new).astype(BF16)
        acc = alpha * acc + jnp.dot(vt, p, preferred_element_type=F32)
        return m_new, acc

    m0 = jnp.full((1, tq), NEG_BIG, F32)
    acc0 = jnp.zeros((VT_ROWS, tq), F32)
    m, acc = tile(km_ref[0], vtm_ref[0], m0, acc0)

    def body(i, carry):
        off = pl.multiple_of(i * tk, tk)
        return tile(k_ref[0, pl.ds(off, tk), :], vt_ref[0, :, pl.ds(off, tk)], *carry)

    m, acc = lax.fori_loop(0, nk, body, (m, acc))
    o_ref[...] = (acc[0:MLA_V_DIM] / acc[MLA_V_DIM:MLA_V_DIM + 1]).astype(BF16)


def _mla_call(qm, km, vt, km_meta, vt_meta, tq, tk):
    n = qm.shape[1]
    return pl.pallas_call(
        functools.partial(_mla_kernel, tk=tk, nk=n // tk),
        grid=(MLA_HEADS, n // tq),
        in_specs=[pl.BlockSpec((1, tq, LANES), lambda h, i: (h, i, 0)),
                  pl.BlockSpec((1, n, LANES), lambda h, i: (h, 0, 0)),
                  pl.BlockSpec((1, VT_ROWS, n), lambda h, i: (h, 0, 0)),
                  pl.BlockSpec((1, LANES, LANES), lambda h, i: (h, 0, 0)),
                  pl.BlockSpec((1, VT_ROWS, LANES), lambda h, i: (h, 0, 0))],
        out_specs=pl.BlockSpec((MLA_V_DIM, tq), lambda h, i: (h, i)),
        out_shape=jax.ShapeDtypeStruct((MLA_WIDTH, n), BF16),
        compiler_params=pltpu.CompilerParams(dimension_semantics=("arbitrary", "arbitrary"),
                                             vmem_limit_bytes=VMEM_LIMIT),
        name="mla",
    )(qm, km, vt, km_meta, vt_meta)


def _post_kernel(x_ref, ya_ref, yt_ref, w_oa_ref, w_ob_ref, g_ffn_ref, rw_hi_ref, rw_lo_ref, rb_ref,
                 h1_ref, m_ref, ids_ref, rank_ref, gate_ref, cnt_ref, base_ref):
    tm = x_ref.shape[0]

    @pl.when(pl.program_id(0) == 0)
    def _():
        base_ref[...] = jnp.zeros_like(base_ref)

    yb = yt_ref[...].T
    h1 = (x_ref[...]
          + jnp.dot(ya_ref[...], w_oa_ref[...], preferred_element_type=F32)
          + jnp.dot(yb, w_ob_ref[...], preferred_element_type=F32))
    h1_ref[...] = h1
    m = _rms(h1, D_MODEL) * g_ffn_ref[...]
    m_ref[...] = m

    m_hi = m.astype(BF16)
    m_lo = (m - m_hi.astype(F32)).astype(BF16)
    rw_hi = rw_hi_ref[...]
    logits = (jnp.dot(m_hi, rw_hi, preferred_element_type=F32)
              + jnp.dot(m_lo, rw_hi, preferred_element_type=F32)
              + jnp.dot(m_hi, rw_lo_ref[...], preferred_element_type=F32)
              + rb_ref[...])

    lane = lax.broadcasted_iota(jnp.int32, (tm, LANES), 1)
    lane_f = lane.astype(F32)
    work = logits
    vals, idxs = [], []
    for _k in range(TOP_K):
        mx = jnp.max(work, axis=-1, keepdims=True)
        ix = jnp.min(jnp.where(work == mx, lane_f, float(LANES)), axis=-1, keepdims=True)
        vals.append(mx)
        idxs.append(ix)
        work = jnp.where(lane_f == ix, -3e38, work)
    exps = [jnp.exp(v - vals[0]) for v in vals]
    den = exps[0] + exps[1] + exps[2] + exps[3]

    hits = [lane_f == ix for ix in idxs]
    onehot = jnp.zeros((tm, LANES), F32)
    for hk in hits:
        onehot = onehot + jnp.where(hk, 1.0, 0.0)
    row = lax.broadcasted_iota(jnp.int32, (tm, tm), 0)
    col = lax.broadcasted_iota(jnp.int32, (tm, tm), 1)
    tri = jnp.where(row > col, 1.0, 0.0).astype(BF16)
    before = jnp.dot(tri, onehot.astype(BF16), preferred_element_type=F32) + base_ref[...]
    ids = jnp.zeros((tm, LANES), F32)
    rank = jnp.zeros((tm, LANES), F32)
    gate = jnp.zeros((tm, LANES), F32)
    for k in range(TOP_K):
        rk = jnp.sum(jnp.where(hits[k], before, 0.0), axis=-1, keepdims=True)
        sel = lane == k
        ids = jnp.where(sel, idxs[k], ids)
        rank = jnp.where(sel, rk, rank)
        gate = jnp.where(sel, exps[k] / den, gate)
    ids_ref[...] = ids.astype(jnp.int32)
    rank_ref[...] = rank.astype(jnp.int32)
    gate_ref[...] = gate
    base = base_ref[...] + jnp.sum(onehot, axis=0, keepdims=True)
    base_ref[...] = base
    cnt_ref[...] = base


def _post_call(x, ya, yt, wts, tm):
    n = x.shape[0]
    w_names = ("w_oa", "w_ob", "g_ffn", "rw_hi", "rw_lo", "rb")
    w_args = [wts[k] for k in w_names]

    def full(a):
        return pl.BlockSpec(a.shape, lambda i: (0,) * a.ndim)

    row = lambda w: pl.BlockSpec((tm, w), lambda i: (i, 0))
    return pl.pallas_call(
        _post_kernel, grid=(n // tm,),
        in_specs=[row(D_MODEL), row(SWA_WIDTH), pl.BlockSpec((MLA_WIDTH, tm), lambda i: (0, i))]
        + [full(a) for a in w_args],
        out_specs=(row(D_MODEL), row(D_MODEL), row(LANES), row(LANES), row(LANES),
                   pl.BlockSpec((1, LANES), lambda i: (0, 0))),
        out_shape=(jax.ShapeDtypeStruct((n, D_MODEL), F32), jax.ShapeDtypeStruct((n, D_MODEL), F32),
                   jax.ShapeDtypeStruct((n, LANES), jnp.int32), jax.ShapeDtypeStruct((n, LANES), jnp.int32),
                   jax.ShapeDtypeStruct((n, LANES), F32), jax.ShapeDtypeStruct((1, LANES), F32)),
        scratch_shapes=[pltpu.VMEM((1, LANES), F32)],
        compiler_params=pltpu.CompilerParams(dimension_semantics=("arbitrary",), vmem_limit_bytes=VMEM_LIMIT),
        name="post",
    )(x, ya, yt, *w_args)


def _row_copy(src, src_row, dst, dst_row, sem):
    return pltpu.make_async_copy(src.at[pl.ds(src_row, 1), :], dst.at[pl.ds(dst_row, 1), :], sem)


def _dispatch_kernel(dest_ref, m_ref, xs_in_ref, xs_ref, sem, *, td):
    del xs_in_ref
    base = pl.program_id(0) * (td * TOP_K)

    def issue(t, carry):
        for k in range(TOP_K):
            _row_copy(m_ref, t, xs_ref, dest_ref[base + t * TOP_K + k], sem).start()
        return carry

    lax.fori_loop(0, td, issue, 0)

    def drain(t, carry):
        for _k in range(TOP_K):
            _row_copy(m_ref, 0, xs_ref, 0, sem).wait()
        return carry

    lax.fori_loop(0, td, drain, 0)


def _dispatch_call(dest_flat, m, n_slots, td):
    n = m.shape[0]
    xs0 = jnp.zeros((n_slots, D_MODEL), F32)
    grid_spec = pltpu.PrefetchScalarGridSpec(
        num_scalar_prefetch=1, grid=(n // td,),
        in_specs=[pl.BlockSpec((td, D_MODEL), lambda i, d: (i, 0)),
                  pl.BlockSpec(memory_space=pl.ANY)],
        out_specs=pl.BlockSpec(memory_space=pl.ANY),
        scratch_shapes=[pltpu.SemaphoreType.DMA])
    return pl.pallas_call(
        functools.partial(_dispatch_kernel, td=td), grid_spec=grid_spec,
        out_shape=jax.ShapeDtypeStruct((n_slots, D_MODEL), F32),
        input_output_aliases={2: 0},
        compiler_params=pltpu.CompilerParams(dimension_semantics=("arbitrary",), vmem_limit_bytes=VMEM_LIMIT),
        name="dispatch",
    )(dest_flat, m, xs0)


def _ffn_kernel(be_ref, nused_ref, xs_ref, wg_ref, bg_ref, wu_ref, bu_ref, wd_ref, bd_ref, ys_ref,
                wg_s, wu_s, wd_s, *, ff_chunk):
    b = pl.program_id(0)
    prev = be_ref[jnp.maximum(b - 1, 0)]
    fresh = (b == 0) | (be_ref[b] != prev)

    @pl.when(fresh & (b < nused_ref[0]))
    def _():
        wg_s[...] = wg_ref[0].astype(BF16)
        wu_s[...] = wu_ref[0].astype(BF16)
        wd_s[...] = wd_ref[0].astype(BF16)

    @pl.when(b < nused_ref[0])
    def _():
        x = xs_ref[...].astype(BF16)
        y = jnp.zeros(ys_ref.shape, F32) + bd_ref[0]
        for c in range(D_FF // ff_chunk):
            sl = slice(c * ff_chunk, (c + 1) * ff_chunk)
            g = jnp.dot(x, wg_s[:, sl], preferred_element_type=F32) + bg_ref[0][:, sl]
            u = jnp.dot(x, wu_s[:, sl], preferred_element_type=F32) + bu_ref[0][:, sl]
            g = jnp.minimum(g, SWIGLU_LIMIT)
            u = jnp.clip(u, -SWIGLU_LIMIT, SWIGLU_LIMIT)
            act = (u + 1.0) * (g * jax.nn.sigmoid(SWIGLU_ALPHA * g))
            y = y + jnp.dot(act.astype(BF16), wd_s[sl, :], preferred_element_type=F32)
        ys_ref[...] = y

    @pl.when(b >= nused_ref[0])
    def _():
        ys_ref[...] = jnp.zeros_like(ys_ref)


def _ffn_call(block_e, n_used, xs, w_gate, b_gate, w_up, b_up, w_down, b_down, te):
    n_slots = xs.shape[0]
    wspec = lambda r, c: pl.BlockSpec((1, r, c), lambda b, be, nu: (be[b], 0, 0))
    grid_spec = pltpu.PrefetchScalarGridSpec(
        num_scalar_prefetch=2, grid=(n_slots // te,),
        in_specs=[pl.BlockSpec((te, D_MODEL), lambda b, be, nu: (b, 0)),
                  wspec(D_MODEL, D_FF), wspec(1, D_FF), wspec(D_MODEL, D_FF), wspec(1, D_FF),
                  wspec(D_FF, D_MODEL), wspec(1, D_MODEL)],
        out_specs=pl.BlockSpec((te, D_MODEL), lambda b, be, nu: (b, 0)),
        scratch_shapes=[pltpu.VMEM((D_MODEL, D_FF), BF16), pltpu.VMEM((D_MODEL, D_FF), BF16),
                        pltpu.VMEM((D_FF, D_MODEL), BF16)])
    return pl.pallas_call(
        functools.partial(_ffn_kernel, ff_chunk=512), grid_spec=grid_spec,
        out_shape=jax.ShapeDtypeStruct((n_slots, D_MODEL), F32),
        compiler_params=pltpu.CompilerParams(dimension_semantics=("arbitrary",), vmem_limit_bytes=VMEM_LIMIT),
        name="ffn",
    )(block_e, n_used, xs, w_gate, b_gate[:, None, :], w_up, b_up[:, None, :], w_down, b_down[:, None, :])


def _combine_kernel(dest_ref, h1_ref, gate_ref, ys_ref, o_ref, buf, sem, *, tc):
    base = pl.program_id(0) * (tc * TOP_K)

    def issue(t, carry):
        for k in range(TOP_K):
            _row_copy(ys_ref, dest_ref[base + t * TOP_K + k], buf.at[k], t, sem).start()
        return carry

    lax.fori_loop(0, tc, issue, 0)

    def drain(t, carry):
        for k in range(TOP_K):
            _row_copy(ys_ref, 0, buf.at[k], 0, sem).wait()
        return carry

    lax.fori_loop(0, tc, drain, 0)
    gate = gate_ref[...]
    out = h1_ref[...]
    for k in range(TOP_K):
        out = out + gate[:, k:k + 1] * buf[k]
    o_ref[...] = out


def _combine_call(dest_flat, h1, gate, ys, tc):
    n = h1.shape[0]
    grid_spec = pltpu.PrefetchScalarGridSpec(
        num_scalar_prefetch=1, grid=(n // tc,),
        in_specs=[pl.BlockSpec((tc, D_MODEL), lambda i, d: (i, 0)),
                  pl.BlockSpec((tc, LANES), lambda i, d: (i, 0)),
                  pl.BlockSpec(memory_space=pl.ANY)],
        out_specs=pl.BlockSpec((tc, D_MODEL), lambda i, d: (i, 0)),
        scratch_shapes=[pltpu.VMEM((TOP_K, tc, D_MODEL), F32), pltpu.SemaphoreType.DMA])
    return pl.pallas_call(
        functools.partial(_combine_kernel, tc=tc), grid_spec=grid_spec,
        out_shape=jax.ShapeDtypeStruct((n, D_MODEL), F32),
        compiler_params=pltpu.CompilerParams(dimension_semantics=("arbitrary",), vmem_limit_bytes=VMEM_LIMIT),
        name="combine",
    )(dest_flat, h1, gate, ys)


def _prepare_weights(attn_norm_g, w_in, swa_q_norm_g, swa_k_norm_g, mla_q_lat_norm_g, w_mla_q_up,
                     mla_kv_lat_norm_g, w_mla_kv_up, mla_q_norm_g, mla_k_norm_g, w_out, ffn_norm_g,
                     router_w, router_b):
    o_k = SWA_WIDTH
    o_v = o_k + SWA_KV_WIDTH
    o_ql = o_v + SWA_KV_WIDTH
    o_kv = o_ql + MLA_Q_LORA
    o_kr = o_kv + MLA_KV_LORA
    perm = np.arange(SWA_HEADS).reshape(SWA_KV_HEADS, SWA_GROUP).T.reshape(-1)
    w_qs = w_in[:, :o_k].reshape(D_MODEL, SWA_HEADS, SWA_HEAD_DIM)[:, perm].reshape(D_MODEL, SWA_WIDTH)
    w_kr = jnp.zeros((D_MODEL, LANES), F32).at[:, MLA_NOPE_DIM:MLA_QK_DIM].set(w_in[:, o_kr:])
    w_in_r = jnp.concatenate([w_qs, w_in[:, o_k:o_kr], w_kr], axis=1).astype(BF16)

    pad_qk = LANES - MLA_QK_DIM
    w_qup = jnp.pad(w_mla_q_up.reshape(MLA_Q_LORA, MLA_HEADS, MLA_QK_DIM), ((0, 0), (0, 0), (0, pad_qk)))
    w_qup = w_qup.reshape(MLA_Q_LORA, MLA_HEADS * LANES).astype(BF16)
    kv_up = w_mla_kv_up.reshape(MLA_KV_LORA, MLA_HEADS, MLA_NOPE_DIM + MLA_V_DIM)
    w_k = jnp.pad(kv_up[:, :, :MLA_NOPE_DIM], ((0, 0), (0, 0), (0, LANES - MLA_NOPE_DIM)))
    w_k = w_k.reshape(MLA_KV_LORA, MLA_HEADS * LANES).astype(BF16)
    w_vt = jnp.transpose(kv_up[:, :, MLA_NOPE_DIM:], (1, 2, 0))
    w_vt = jnp.pad(w_vt, ((0, 0), (0, VT_ROWS - MLA_V_DIM), (0, 0))).reshape(MLA_HEADS * VT_ROWS, MLA_KV_LORA)
    ones = np.zeros((MLA_HEADS, VT_ROWS, 1), np.float32)
    ones[:, MLA_V_DIM, 0] = 1.0

    w_oa = w_out[:SWA_WIDTH].reshape(SWA_HEADS, SWA_HEAD_DIM, D_MODEL)[perm].reshape(SWA_WIDTH, D_MODEL)
    rw = jnp.pad(router_w, ((0, 0), (0, LANES - N_EXPERTS)))
    rw_hi = rw.astype(BF16)
    rw_lo = (rw - rw_hi.astype(F32)).astype(BF16)
    rb = jnp.concatenate([router_b.astype(F32), jnp.full((LANES - N_EXPERTS,), NEG_BIG, F32)])[None, :]
    return {
        "g_attn": attn_norm_g[None, :], "w_in": w_in_r,
        "g_qs": jnp.tile(swa_q_norm_g, 2)[None, :] * (SWA_HEAD_DIM ** -0.5),
        "g_ks": jnp.tile(swa_k_norm_g, 2)[None, :],
        "g_qlat": mla_q_lat_norm_g[None, :], "w_qup": w_qup,
        "g_kvlat": mla_kv_lat_norm_g[None, :], "w_k": w_k, "w_vt": w_vt.astype(BF16),
        "ones": jnp.asarray(ones.reshape(MLA_HEADS * VT_ROWS, 1)),
        "g_qm": jnp.pad(mla_q_norm_g, (0, pad_qk))[None, :] * (MLA_QK_DIM ** -0.5),
        "g_km": jnp.pad(mla_k_norm_g, (0, pad_qk))[None, :],
        "w_oa": w_oa.astype(BF16), "w_ob": w_out[SWA_WIDTH:].astype(BF16),
        "g_ffn": ffn_norm_g[None, :], "rw_hi": rw_hi, "rw_lo": rw_lo, "rb": rb,
    }


def _rope_tables(n_rows):
    half = MLA_ROPE_DIM // 2
    inv = 1.0 / (ROPE_THETA ** (jnp.arange(half, dtype=F32) / half))
    ang = jnp.arange(n_rows).astype(F32)[:, None] * inv[None, :]
    cos, sin = jnp.cos(ang), jnp.sin(ang)
    one = jnp.ones((n_rows, MLA_NOPE_DIM), F32)
    zero = jnp.zeros((n_rows, MLA_NOPE_DIM), F32)
    pad1 = jnp.ones((n_rows, LANES - MLA_QK_DIM), F32)
    pad0 = jnp.zeros((n_rows, LANES - MLA_QK_DIM), F32)
    return (jnp.concatenate([one, cos, cos, pad1], axis=1),
            jnp.concatenate([zero, -sin, sin, pad0], axis=1))


def kernel(x, meta_tokens, attn_norm_g, w_in, swa_q_norm_g, swa_k_norm_g, swa_sink, mla_q_lat_norm_g, w_mla_q_up, mla_kv_lat_norm_g, w_mla_kv_up, mla_q_norm_g, mla_k_norm_g, w_out, ffn_norm_g, router_w, router_b, w_gate, b_gate, w_up, b_up, w_down, b_down):
    bsz, n, d = x.shape
    assert bsz == 1 and d == D_MODEL and n % TK_MLA == 0 and attn_norm_g.shape[0] == 1
    wts = _prepare_weights(attn_norm_g[0], w_in[0], swa_q_norm_g[0], swa_k_norm_g[0], mla_q_lat_norm_g[0],
                           w_mla_q_up[0], mla_kv_lat_norm_g[0], w_mla_kv_up[0], mla_q_norm_g[0],
                           mla_k_norm_g[0], w_out[0], ffn_norm_g[0], router_w[0], router_b[0])
    xr = x[0]
    cos_t, sin_t = _rope_tables(N_META + n)

    xm = jnp.pad(meta_tokens.astype(F32), ((0, LANES - N_META), (0, 0)))
    valid_m = (jnp.arange(LANES) < N_META).astype(F32)[None, :]
    _, ks_m, vs_m, _, km_m, vt_m = _proj_call(xm, valid_m, cos_t[:LANES], sin_t[:LANES], wts, LANES)
    qs, ks, vs, qm, km, vt = _proj_call(xr, jnp.ones((1, n), F32), cos_t[N_META:], sin_t[N_META:], wts, TM_PROJ)

    ya = _swa_call(swa_sink[0], qs, ks, vs, ks_m, vs_m)
    yt = _mla_call(qm, km, vt, km_m, vt_m, TQ_MLA, TK_MLA)
    h1, m, ids, rank, gate, cnt = _post_call(xr, ya, yt, wts, TM_POST)

    te = TE_FFN
    counts = cnt[0, :N_EXPERTS].astype(jnp.int32)
    padded = (counts + te - 1) // te * te
    pad_end = jnp.cumsum(padded)
    pad_start = pad_end - padded
    idx = ids[:, :TOP_K]
    dest = (pad_start[idx] + rank[:, :TOP_K]).reshape(-1)
    n_blocks = (n * TOP_K + N_EXPERTS * (te - 1)) // te
    block_e = jnp.minimum(jnp.searchsorted(pad_end, jnp.arange(n_blocks) * te, side="right"),
                          N_EXPERTS - 1).astype(jnp.int32)
    n_used = (pad_end[-1:] // te).astype(jnp.int32)

    xs = _dispatch_call(dest, m, n_blocks * te, TD_DISPATCH)
    ys = _ffn_call(block_e, n_used, xs, w_gate[0], b_gate[0], w_up[0], b_up[0], w_down[0], b_down[0], te)
    out = _combine_call(dest, h1, gate, ys, TC_COMBINE)
    return out[None]
```

```python
import functools

import numpy as np
import jax
import jax.numpy as jnp
from jax import lax
from jax.experimental import pallas as pl
from jax.experimental.pallas import tpu as pltpu

F32 = jnp.float32
BF16 = jnp.bfloat16

D_MODEL = 1024
N_META = 16
BLOCK = 128
WINDOW = 128
NORM_EPS = 1e-6

SWA_HEADS = 8
SWA_KV_HEADS = 2
SWA_GROUP = SWA_HEADS // SWA_KV_HEADS
SWA_HEAD_DIM = 64
SWA_WIDTH = SWA_HEADS * SWA_HEAD_DIM
SWA_KV_WIDTH = SWA_KV_HEADS * SWA_HEAD_DIM

MLA_HEADS = 8
MLA_Q_LORA = 256
MLA_KV_LORA = 128
MLA_NOPE_DIM = 64
MLA_ROPE_DIM = 32
MLA_V_DIM = 64
MLA_QK_DIM = MLA_NOPE_DIM + MLA_ROPE_DIM
MLA_WIDTH = MLA_HEADS * MLA_V_DIM
ROPE_THETA = 10000.0

N_EXPERTS = 32
TOP_K = 4
D_FF = 1024
SWIGLU_LIMIT = 7.0
SWIGLU_ALPHA = 1.702

LANES = 128
VT_ROWS = 80
NEG_BIG = -1e30
LOG2_E = 1.4426950408889634
VMEM_LIMIT = 48 * 1024 * 1024

TM_PROJ = 256
TQ_MLA = 512
TK_MLA = 512
MLA_UNROLL = 4
TM_POST = 256
TD_DISPATCH = 128
TE_FFN = 256
TC_COMBINE = 128

_NT = (((1,), (1,)), ((), ()))


def _rms(v, n):
    return v * lax.rsqrt(jnp.sum(v * v, axis=-1, keepdims=True) * (1.0 / n) + NORM_EPS)


def _proj_kernel(x_ref, valid_ref, cos_ref, sin_ref, g_attn_ref, w_in_ref, g_qs_ref, g_ks_ref,
                 g_qlat_ref, w_qup_ref, g_kvlat_ref, w_k_ref, w_vt_ref, ones_ref, g_qm_ref, g_km_ref,
                 qs_ref, ks_ref, vs_ref, qm_ref, km_ref, vt_ref):
    tm = x_ref.shape[0]
    a = _rms(x_ref[...], D_MODEL) * g_attn_ref[...]
    proj = jnp.dot(a.astype(BF16), w_in_ref[...], preferred_element_type=F32)

    lane = lax.broadcasted_iota(jnp.int32, (tm, LANES), 1)
    lo = lane < SWA_HEAD_DIM

    def seg_norm(v, g):
        v2 = v * v
        s_all = jnp.sum(v2, axis=-1, keepdims=True)
        s_lo = jnp.sum(jnp.where(lo, v2, 0.0), axis=-1, keepdims=True)
        ms = jnp.where(lo, s_lo, s_all - s_lo) * (1.0 / SWA_HEAD_DIM)
        return v * lax.rsqrt(ms + NORM_EPS) * g

    g_qs = g_qs_ref[...]
    for j in range(SWA_WIDTH // LANES):
        sl = slice(j * LANES, (j + 1) * LANES)
        qs_ref[:, sl] = seg_norm(proj[:, sl], g_qs).astype(BF16)
    o_k = SWA_WIDTH
    o_v = o_k + SWA_KV_WIDTH
    o_ql = o_v + SWA_KV_WIDTH
    o_kv = o_ql + MLA_Q_LORA
    o_kr = o_kv + MLA_KV_LORA
    ks_ref[...] = seg_norm(proj[:, o_k:o_v], g_ks_ref[...]).astype(BF16)
    vs_ref[...] = proj[:, o_v:o_ql].astype(BF16)

    cosv = cos_ref[...]
    sinv = sin_ref[...]
    first_half = lane < (MLA_NOPE_DIM + MLA_ROPE_DIM // 2)

    def norm_rope(v, g):
        ms = jnp.sum(v * v, axis=-1, keepdims=True) * (1.0 / MLA_QK_DIM)
        vn = v * lax.rsqrt(ms + NORM_EPS) * g
        rot = jnp.where(first_half,
                        pltpu.roll(vn, LANES - MLA_ROPE_DIM // 2, 1),
                        pltpu.roll(vn, MLA_ROPE_DIM // 2, 1))
        return vn * cosv + rot * sinv

    qln = _rms(proj[:, o_ql:o_kv], MLA_Q_LORA) * g_qlat_ref[...]
    qup = jnp.dot(qln.astype(BF16), w_qup_ref[...], preferred_element_type=F32)
    g_qm = g_qm_ref[...]
    for h in range(MLA_HEADS):
        qm_ref[h] = norm_rope(qup[:, h * LANES:(h + 1) * LANES], g_qm).astype(BF16)

    kvn = (_rms(proj[:, o_kv:o_kr], MLA_KV_LORA) * g_kvlat_ref[...]).astype(BF16)
    knope = jnp.dot(kvn, w_k_ref[...], preferred_element_type=F32)
    krope = proj[:, o_kr:o_kr + LANES]
    g_km = g_km_ref[...]
    for h in range(MLA_HEADS):
        km_ref[h] = norm_rope(knope[:, h * LANES:(h + 1) * LANES] + krope, g_km).astype(BF16)

    vt = lax.dot_general(w_vt_ref[...], kvn, _NT, preferred_element_type=F32)
    vt = vt + ones_ref[...] * valid_ref[...]
    for h in range(MLA_HEADS):
        vt_ref[h] = vt[h * VT_ROWS:(h + 1) * VT_ROWS].astype(BF16)


def _proj_call(x, valid, cos_t, sin_t, wts, tm):
    n = x.shape[0]
    w_names = ("g_attn", "w_in", "g_qs", "g_ks", "g_qlat", "w_qup", "g_kvlat", "w_k", "w_vt", "ones", "g_qm", "g_km")
    w_args = [wts[k] for k in w_names]

    def full(a):
        return pl.BlockSpec(a.shape, lambda i: (0,) * a.ndim)

    in_specs = [pl.BlockSpec((tm, D_MODEL), lambda i: (i, 0)),
                pl.BlockSpec((1, tm), lambda i: (0, i)),
                pl.BlockSpec((tm, LANES), lambda i: (i, 0)),
                pl.BlockSpec((tm, LANES), lambda i: (i, 0))] + [full(a) for a in w_args]
    out_shape = (jax.ShapeDtypeStruct((n, SWA_WIDTH), BF16),
                 jax.ShapeDtypeStruct((n, SWA_KV_WIDTH), BF16),
                 jax.ShapeDtypeStruct((n, SWA_KV_WIDTH), BF16),
                 jax.ShapeDtypeStruct((MLA_HEADS, n, LANES), BF16),
                 jax.ShapeDtypeStruct((MLA_HEADS, n, LANES), BF16),
                 jax.ShapeDtypeStruct((MLA_HEADS, VT_ROWS, n), BF16))
    out_specs = (pl.BlockSpec((tm, SWA_WIDTH), lambda i: (i, 0)),
                 pl.BlockSpec((tm, SWA_KV_WIDTH), lambda i: (i, 0)),
                 pl.BlockSpec((tm, SWA_KV_WIDTH), lambda i: (i, 0)),
                 pl.BlockSpec((MLA_HEADS, tm, LANES), lambda i: (0, i, 0)),
                 pl.BlockSpec((MLA_HEADS, tm, LANES), lambda i: (0, i, 0)),
                 pl.BlockSpec((MLA_HEADS, VT_ROWS, tm), lambda i: (0, 0, i)))
    return pl.pallas_call(
        _proj_kernel, grid=(n // tm,), in_specs=in_specs, out_specs=out_specs, out_shape=out_shape,
        compiler_params=pltpu.CompilerParams(dimension_semantics=("arbitrary",), vmem_limit_bytes=VMEM_LIMIT),
        name="proj",
    )(x, valid, cos_t, sin_t, *w_args)


def _swa_kernel(sink_ref, q_ref, kp_ref, kc_ref, kn_ref, vp_ref, vc_ref, vn_ref, km_ref, vm_ref, o_ref, *, slopes):
    b = pl.program_id(0)
    nb = pl.num_programs(0)
    nkeys = 3 * BLOCK + LANES
    kall = jnp.concatenate([kp_ref[...], kc_ref[...], kn_ref[...], km_ref[...]], axis=0)
    vall = jnp.concatenate([vp_ref[...], vc_ref[...], vn_ref[...], vm_ref[...]], axis=0)

    r = lax.broadcasted_iota(jnp.int32, (BLOCK, nkeys), 0)
    c = lax.broadcasted_iota(jnp.int32, (BLOCK, nkeys), 1)
    d = jnp.abs(BLOCK + r - c)
    lo_edge = jnp.where(b > 0, 0, BLOCK)
    hi_edge = jnp.where(b < nb - 1, 3 * BLOCK, 2 * BLOCK)
    in_win = (d <= WINDOW) & (c >= lo_edge) & (c < hi_edge)
    is_meta = (c >= 3 * BLOCK) & (c < 3 * BLOCK + N_META)
    valid = in_win | is_meta
    negdist = jnp.where(c < 3 * BLOCK, -d.astype(F32), 0.0)

    lane = lax.broadcasted_iota(jnp.int32, (BLOCK, LANES), 1)
    lo = lane < SWA_HEAD_DIM
    for j in range(SWA_GROUP):
        qg = q_ref[:, j * LANES:(j + 1) * LANES].astype(F32)
        q2 = jnp.concatenate([jnp.where(lo, qg, 0.0), jnp.where(lo, 0.0, qg)], axis=0).astype(BF16)
        s = lax.dot_general(q2, kall, _NT, preferred_element_type=F32)
        outs = []
        for half, h in ((0, j), (1, j + SWA_GROUP)):
            sh = s[half * BLOCK:(half + 1) * BLOCK]
            sh = jnp.where(valid, sh + slopes[h] * negdist, NEG_BIG)
            sink = sink_ref[h]
            m = jnp.maximum(jnp.max(sh, axis=-1, keepdims=True), sink)
            p = jnp.exp(sh - m)
            den = jnp.sum(p, axis=-1, keepdims=True) + jnp.exp(sink - m)
            o = jnp.dot(p.astype(BF16), vall, preferred_element_type=F32)
            outs.append(o / den)
        o_ref[:, j * LANES:(j + 1) * LANES] = jnp.where(lo, outs[0], outs[1]).astype(BF16)


def _swa_call(sink, qs, ks, vs, ks_meta, vs_meta):
    n = qs.shape[0]
    nb = n // BLOCK
    slopes = tuple(float(v) for v in 2.0 ** (-8.0 * np.arange(1, SWA_HEADS + 1) / SWA_HEADS))
    kv_prev = pl.BlockSpec((BLOCK, SWA_KV_WIDTH), lambda b: (jnp.maximum(b - 1, 0), 0))
    kv_cur = pl.BlockSpec((BLOCK, SWA_KV_WIDTH), lambda b: (b, 0))
    kv_next = pl.BlockSpec((BLOCK, SWA_KV_WIDTH), lambda b: (jnp.minimum(b + 1, nb - 1), 0))
    meta = pl.BlockSpec((LANES, SWA_KV_WIDTH), lambda b: (0, 0))
    return pl.pallas_call(
        functools.partial(_swa_kernel, slopes=slopes),
        grid=(nb,),
        in_specs=[pl.BlockSpec(memory_space=pltpu.SMEM),
                  pl.BlockSpec((BLOCK, SWA_WIDTH), lambda b: (b, 0)),
                  kv_prev, kv_cur, kv_next, kv_prev, kv_cur, kv_next, meta, meta],
        out_specs=pl.BlockSpec((BLOCK, SWA_WIDTH), lambda b: (b, 0)),
        out_shape=jax.ShapeDtypeStruct((n, SWA_WIDTH), BF16),
        compiler_params=pltpu.CompilerParams(dimension_semantics=("arbitrary",), vmem_limit_bytes=VMEM_LIMIT),
        name="swa",
    )(sink, qs, ks, ks, ks, vs, vs, vs, ks_meta, vs_meta)


def _mla_kernel(q_ref, k_ref, vt_ref, km_ref, vtm_ref, o_ref, s_scr, *, tk, nk, unroll):
    q = q_ref[0]
    tq = q.shape[0]

    def scores(off):
        return lax.dot_general(k_ref[0, pl.ds(off, tk), :], q, _NT, preferred_element_type=F32)

    def soft_pv(s, vt, m, acc):
        m_new = jnp.maximum(m, jnp.max(s, axis=0, keepdims=True))
        alpha = jnp.exp2(m - m_new)
        p = jnp.exp2(s - m_new).astype(BF16)
        acc = alpha * acc + jnp.dot(vt, p, preferred_element_type=F32)
        return m_new, acc

    m0 = jnp.full((1, tq), NEG_BIG, F32)
    acc0 = jnp.zeros((VT_ROWS, tq), F32)
    s_meta = lax.dot_general(km_ref[0], q, _NT, preferred_element_type=F32)
    s_scr[0] = scores(0)
    m, acc = soft_pv(s_meta, vtm_ref[0], m0, acc0)

    def body(i, carry):
        m, acc = carry
        for u in range(unroll):
            t = i * unroll + u
            cur = u % 2
            s_next = scores(pl.multiple_of(jnp.minimum(t + 1, nk - 1) * tk, tk))
            off = pl.multiple_of(t * tk, tk)
            m, acc = soft_pv(s_scr[cur], vt_ref[0, :, pl.ds(off, tk)], m, acc)
            s_scr[1 - cur] = s_next
        return m, acc

    m, acc = lax.fori_loop(0, nk // unroll, body, (m, acc))
    o_ref[...] = (acc[0:MLA_V_DIM] / acc[MLA_V_DIM:MLA_V_DIM + 1]).astype(BF16)


def _mla_call(qm, km, vt, km_meta, vt_meta, tq, tk):
    n = qm.shape[1]
    nk = n // tk
    unroll = MLA_UNROLL if nk % MLA_UNROLL == 0 else 2
    assert nk % unroll == 0
    return pl.pallas_call(
        functools.partial(_mla_kernel, tk=tk, nk=nk, unroll=unroll),
        grid=(MLA_HEADS, n // tq),
        in_specs=[pl.BlockSpec((1, tq, LANES), lambda h, i: (h, i, 0)),
                  pl.BlockSpec((1, n, LANES), lambda h, i: (h, 0, 0)),
                  pl.BlockSpec((1, VT_ROWS, n), lambda h, i: (h, 0, 0)),
                  pl.BlockSpec((1, LANES, LANES), lambda h, i: (h, 0, 0)),
                  pl.BlockSpec((1, VT_ROWS, LANES), lambda h, i: (h, 0, 0))],
        out_specs=pl.BlockSpec((MLA_V_DIM, tq), lambda h, i: (h, i)),
        out_shape=jax.ShapeDtypeStruct((MLA_WIDTH, n), BF16),
        scratch_shapes=[pltpu.VMEM((2, tk, tq), F32)],
        compiler_params=pltpu.CompilerParams(dimension_semantics=("arbitrary", "arbitrary"),
                                             vmem_limit_bytes=VMEM_LIMIT),
        name="mla",
    )(qm, km, vt, km_meta, vt_meta)


def _post_kernel(x_ref, ya_ref, yt_ref, w_oa_ref, w_ob_ref, g_ffn_ref, rw_hi_ref, rw_lo_ref, rb_ref,
                 h1_ref, m_ref, ids_ref, rank_ref, gate_ref, cnt_ref, base_ref):
    tm = x_ref.shape[0]

    @pl.when(pl.program_id(0) == 0)
    def _():
        base_ref[...] = jnp.zeros_like(base_ref)

    yb = yt_ref[...].T
    h1 = (x_ref[...]
          + jnp.dot(ya_ref[...], w_oa_ref[...], preferred_element_type=F32)
          + jnp.dot(yb, w_ob_ref[...], preferred_element_type=F32))
    h1_ref[...] = h1
    m = _rms(h1, D_MODEL) * g_ffn_ref[...]
    m_ref[...] = m

    m_hi = m.astype(BF16)
    m_lo = (m - m_hi.astype(F32)).astype(BF16)
    rw_hi = rw_hi_ref[...]
    logits = (jnp.dot(m_hi, rw_hi, preferred_element_type=F32)
              + jnp.dot(m_lo, rw_hi, preferred_element_type=F32)
              + jnp.dot(m_hi, rw_lo_ref[...], preferred_element_type=F32)
              + rb_ref[...])

    lane = lax.broadcasted_iota(jnp.int32, (tm, LANES), 1)
    lane_f = lane.astype(F32)
    work = logits
    vals, idxs = [], []
    for _k in range(TOP_K):
        mx = jnp.max(work, axis=-1, keepdims=True)
        ix = jnp.min(jnp.where(work == mx, lane_f, float(LANES)), axis=-1, keepdims=True)
        vals.append(mx)
        idxs.append(ix)
        work = jnp.where(lane_f == ix, -3e38, work)
    exps = [jnp.exp(v - vals[0]) for v in vals]
    den = exps[0] + exps[1] + exps[2] + exps[3]

    hits = [lane_f == ix for ix in idxs]
    onehot = jnp.zeros((tm, LANES), F32)
    for hk in hits:
        onehot = onehot + jnp.where(hk, 1.0, 0.0)
    row = lax.broadcasted_iota(jnp.int32, (tm, tm), 0)
    col = lax.broadcasted_iota(jnp.int32, (tm, tm), 1)
    tri = jnp.where(row > col, 1.0, 0.0).astype(BF16)
    before = jnp.dot(tri, onehot.astype(BF16), preferred_element_type=F32) + base_ref[...]
    ids = jnp.zeros((tm, LANES), F32)
    rank = jnp.zeros((tm, LANES), F32)
    gate = jnp.zeros((tm, LANES), F32)
    for k in range(TOP_K):
        rk = jnp.sum(jnp.where(hits[k], before, 0.0), axis=-1, keepdims=True)
        sel = lane == k
        ids = jnp.where(sel, idxs[k], ids)
        rank = jnp.where(sel, rk, rank)
        gate = jnp.where(sel, exps[k] / den, gate)
    ids_ref[...] = ids.T[0:8].astype(jnp.int32)
    rank_ref[...] = rank.T[0:8].astype(jnp.int32)
    gate_ref[...] = gate
    base = base_ref[...] + jnp.sum(onehot, axis=0, keepdims=True)
    base_ref[...] = base
    cnt_ref[...] = base


def _post_call(x, ya, yt, wts, tm):
    n = x.shape[0]
    w_names = ("w_oa", "w_ob", "g_ffn", "rw_hi", "rw_lo", "rb")
    w_args = [wts[k] for k in w_names]

    def full(a):
        return pl.BlockSpec(a.shape, lambda i: (0,) * a.ndim)

    row = lambda w: pl.BlockSpec((tm, w), lambda i: (i, 0))
    choice = pl.BlockSpec((8, tm), lambda i: (0, i))
    return pl.pallas_call(
        _post_kernel, grid=(n // tm,),
        in_specs=[row(D_MODEL), row(SWA_WIDTH), pl.BlockSpec((MLA_WIDTH, tm), lambda i: (0, i))]
        + [full(a) for a in w_args],
        out_specs=(row(D_MODEL), row(D_MODEL), choice, choice, row(LANES),
                   pl.BlockSpec((1, LANES), lambda i: (0, 0))),
        out_shape=(jax.ShapeDtypeStruct((n, D_MODEL), F32), jax.ShapeDtypeStruct((n, D_MODEL), F32),
                   jax.ShapeDtypeStruct((8, n), jnp.int32), jax.ShapeDtypeStruct((8, n), jnp.int32),
                   jax.ShapeDtypeStruct((n, LANES), F32), jax.ShapeDtypeStruct((1, LANES), F32)),
        scratch_shapes=[pltpu.VMEM((1, LANES), F32)],
        compiler_params=pltpu.CompilerParams(dimension_semantics=("arbitrary",), vmem_limit_bytes=VMEM_LIMIT),
        name="post",
    )(x, ya, yt, *w_args)


def _row_copy(src, src_row, dst, dst_row, sem):
    return pltpu.make_async_copy(src.at[pl.ds(src_row, 1), :], dst.at[pl.ds(dst_row, 1), :], sem)


def _dispatch_kernel(dest_ref, tail_ref, nused_ref, m_ref, xs_ref, zbuf, sem, zsem, *, td, n, te):
    def zero_copy(row):
        return pltpu.make_async_copy(zbuf, xs_ref.at[pl.ds(pl.multiple_of(row, te), te), :], zsem)

    @pl.when(pl.program_id(0) == 0)
    def _():
        zbuf[...] = jnp.zeros_like(zbuf)
        n_blocks = xs_ref.shape[0] // te
        for e in range(N_EXPERTS):
            @pl.when(tail_ref[e] >= 0)
            def _():
                zero_copy(jnp.maximum(tail_ref[e], 0)).start()

        def start_unused(b, carry):
            zero_copy(b * te).start()
            return carry

        lax.fori_loop(nused_ref[0], n_blocks, start_unused, 0)
        for e in range(N_EXPERTS):
            @pl.when(tail_ref[e] >= 0)
            def _():
                zero_copy(0).wait()

        def wait_unused(b, carry):
            zero_copy(0).wait()
            return carry

        lax.fori_loop(nused_ref[0], n_blocks, wait_unused, 0)

    base = pl.program_id(0) * td

    def issue(t, carry):
        for k in range(TOP_K):
            _row_copy(m_ref, t, xs_ref, dest_ref[k * n + base + t], sem).start()
        return carry

    lax.fori_loop(0, td, issue, 0)

    def drain(t, carry):
        for _k in range(TOP_K):
            _row_copy(m_ref, 0, xs_ref, 0, sem).wait()
        return carry

    lax.fori_loop(0, td, drain, 0)


def _dispatch_call(dest_flat, tail_start, n_used, m, n_slots, td, te):
    n = m.shape[0]
    grid_spec = pltpu.PrefetchScalarGridSpec(
        num_scalar_prefetch=3, grid=(n // td,),
        in_specs=[pl.BlockSpec((td, D_MODEL), lambda i, d, z, u: (i, 0))],
        out_specs=pl.BlockSpec(memory_space=pl.ANY),
        scratch_shapes=[pltpu.VMEM((te, D_MODEL), F32), pltpu.SemaphoreType.DMA, pltpu.SemaphoreType.DMA])
    return pl.pallas_call(
        functools.partial(_dispatch_kernel, td=td, n=n, te=te), grid_spec=grid_spec,
        out_shape=jax.ShapeDtypeStruct((n_slots, D_MODEL), F32),
        compiler_params=pltpu.CompilerParams(dimension_semantics=("arbitrary",), vmem_limit_bytes=VMEM_LIMIT),
        name="dispatch",
    )(dest_flat, tail_start, n_used, m)


def _ffn_kernel(be_ref, nused_ref, xs_ref, wg_ref, bg_ref, wu_ref, bu_ref, wd_ref, bd_ref, ys_ref,
                wg_s, wu_s, wd_s, *, ff_chunk):
    b = pl.program_id(0)
    prev = be_ref[jnp.maximum(b - 1, 0)]
    fresh = (b == 0) | (be_ref[b] != prev)

    @pl.when(fresh & (b < nused_ref[0]))
    def _():
        wg_s[...] = wg_ref[0].astype(BF16)
        wu_s[...] = wu_ref[0].astype(BF16)
        wd_s[...] = wd_ref[0].astype(BF16)

    @pl.when(b < nused_ref[0])
    def _():
        x = xs_ref[...].astype(BF16)
        y = jnp.zeros(ys_ref.shape, F32) + bd_ref[0]
        for c in range(D_FF // ff_chunk):
            sl = slice(c * ff_chunk, (c + 1) * ff_chunk)
            g = jnp.dot(x, wg_s[:, sl], preferred_element_type=F32) + bg_ref[0][:, sl]
            u = jnp.dot(x, wu_s[:, sl], preferred_element_type=F32) + bu_ref[0][:, sl]
            g = jnp.minimum(g, SWIGLU_LIMIT)
            u = jnp.clip(u, -SWIGLU_LIMIT, SWIGLU_LIMIT)
            act = (u + 1.0) * (g * jax.nn.sigmoid(SWIGLU_ALPHA * g))
            y = y + jnp.dot(act.astype(BF16), wd_s[sl, :], preferred_element_type=F32)
        ys_ref[...] = y

    @pl.when(b >= nused_ref[0])
    def _():
        ys_ref[...] = jnp.zeros_like(ys_ref)


def _ffn_call(block_e, n_used, xs, w_gate, b_gate, w_up, b_up, w_down, b_down, te):
    n_slots = xs.shape[0]
    wspec = lambda r, c: pl.BlockSpec((1, r, c), lambda b, be, nu: (be[b], 0, 0))
    grid_spec = pltpu.PrefetchScalarGridSpec(
        num_scalar_prefetch=2, grid=(n_slots // te,),
        in_specs=[pl.BlockSpec((te, D_MODEL), lambda b, be, nu: (jnp.minimum(b, nu[0] - 1), 0)),
                  wspec(D_MODEL, D_FF), wspec(1, D_FF), wspec(D_MODEL, D_FF), wspec(1, D_FF),
                  wspec(D_FF, D_MODEL), wspec(1, D_MODEL)],
        out_specs=pl.BlockSpec((te, D_MODEL), lambda b, be, nu: (b, 0)),
        scratch_shapes=[pltpu.VMEM((D_MODEL, D_FF), BF16), pltpu.VMEM((D_MODEL, D_FF), BF16),
                        pltpu.VMEM((D_FF, D_MODEL), BF16)])
    return pl.pallas_call(
        functools.partial(_ffn_kernel, ff_chunk=512), grid_spec=grid_spec,
        out_shape=jax.ShapeDtypeStruct((n_slots, D_MODEL), F32),
        compiler_params=pltpu.CompilerParams(dimension_semantics=("arbitrary",), vmem_limit_bytes=VMEM_LIMIT),
        name="ffn",
    )(block_e, n_used, xs, w_gate, b_gate[:, None, :], w_up, b_up[:, None, :], w_down, b_down[:, None, :])


def _combine_kernel(dest_ref, h1_ref, gate_ref, ys_ref, o_ref, buf, sem, *, tc, n):
    base = pl.program_id(0) * tc

    def issue(t, carry):
        for k in range(TOP_K):
            _row_copy(ys_ref, dest_ref[k * n + base + t], buf.at[k], t, sem).start()
        return carry

    lax.fori_loop(0, tc, issue, 0)

    def drain(t, carry):
        for k in range(TOP_K):
            _row_copy(ys_ref, 0, buf.at[k], 0, sem).wait()
        return carry

    lax.fori_loop(0, tc, drain, 0)
    gate = gate_ref[...]
    out = h1_ref[...]
    for k in range(TOP_K):
        out = out + gate[:, k:k + 1] * buf[k]
    o_ref[...] = out


def _combine_call(dest_flat, h1, gate, ys, tc):
    n = h1.shape[0]
    grid_spec = pltpu.PrefetchScalarGridSpec(
        num_scalar_prefetch=1, grid=(n // tc,),
        in_specs=[pl.BlockSpec((tc, D_MODEL), lambda i, d: (i, 0)),
                  pl.BlockSpec((tc, LANES), lambda i, d: (i, 0)),
                  pl.BlockSpec(memory_space=pl.ANY)],
        out_specs=pl.BlockSpec((tc, D_MODEL), lambda i, d: (i, 0)),
        scratch_shapes=[pltpu.VMEM((TOP_K, tc, D_MODEL), F32), pltpu.SemaphoreType.DMA])
    return pl.pallas_call(
        functools.partial(_combine_kernel, tc=tc, n=n), grid_spec=grid_spec,
        out_shape=jax.ShapeDtypeStruct((n, D_MODEL), F32),
        compiler_params=pltpu.CompilerParams(dimension_semantics=("arbitrary",), vmem_limit_bytes=VMEM_LIMIT),
        name="combine",
    )(dest_flat, h1, gate, ys)


def _prepare_weights(attn_norm_g, w_in, swa_q_norm_g, swa_k_norm_g, mla_q_lat_norm_g, w_mla_q_up,
                     mla_kv_lat_norm_g, w_mla_kv_up, mla_q_norm_g, mla_k_norm_g, w_out, ffn_norm_g,
                     router_w, router_b):
    o_k = SWA_WIDTH
    o_v = o_k + SWA_KV_WIDTH
    o_ql = o_v + SWA_KV_WIDTH
    o_kv = o_ql + MLA_Q_LORA
    o_kr = o_kv + MLA_KV_LORA
    perm = np.arange(SWA_HEADS).reshape(SWA_KV_HEADS, SWA_GROUP).T.reshape(-1)
    w_qs = w_in[:, :o_k].reshape(D_MODEL, SWA_HEADS, SWA_HEAD_DIM)[:, perm].reshape(D_MODEL, SWA_WIDTH)
    w_kr = jnp.zeros((D_MODEL, LANES), F32).at[:, MLA_NOPE_DIM:MLA_QK_DIM].set(w_in[:, o_kr:])
    w_in_r = jnp.concatenate([w_qs, w_in[:, o_k:o_kr], w_kr], axis=1).astype(BF16)

    pad_qk = LANES - MLA_QK_DIM
    w_qup = jnp.pad(w_mla_q_up.reshape(MLA_Q_LORA, MLA_HEADS, MLA_QK_DIM), ((0, 0), (0, 0), (0, pad_qk)))
    w_qup = w_qup.reshape(MLA_Q_LORA, MLA_HEADS * LANES).astype(BF16)
    kv_up = w_mla_kv_up.reshape(MLA_KV_LORA, MLA_HEADS, MLA_NOPE_DIM + MLA_V_DIM)
    w_k = jnp.pad(kv_up[:, :, :MLA_NOPE_DIM], ((0, 0), (0, 0), (0, LANES - MLA_NOPE_DIM)))
    w_k = w_k.reshape(MLA_KV_LORA, MLA_HEADS * LANES).astype(BF16)
    w_vt = jnp.transpose(kv_up[:, :, MLA_NOPE_DIM:], (1, 2, 0))
    w_vt = jnp.pad(w_vt, ((0, 0), (0, VT_ROWS - MLA_V_DIM), (0, 0))).reshape(MLA_HEADS * VT_ROWS, MLA_KV_LORA)
    ones = np.zeros((MLA_HEADS, VT_ROWS, 1), np.float32)
    ones[:, MLA_V_DIM, 0] = 1.0

    w_oa = w_out[:SWA_WIDTH].reshape(SWA_HEADS, SWA_HEAD_DIM, D_MODEL)[perm].reshape(SWA_WIDTH, D_MODEL)
    rw = jnp.pad(router_w, ((0, 0), (0, LANES - N_EXPERTS)))
    rw_hi = rw.astype(BF16)
    rw_lo = (rw - rw_hi.astype(F32)).astype(BF16)
    rb = jnp.concatenate([router_b.astype(F32), jnp.full((LANES - N_EXPERTS,), NEG_BIG, F32)])[None, :]
    return {
        "g_attn": attn_norm_g[None, :], "w_in": w_in_r,
        "g_qs": jnp.tile(swa_q_norm_g, 2)[None, :] * (SWA_HEAD_DIM ** -0.5),
        "g_ks": jnp.tile(swa_k_norm_g, 2)[None, :],
        "g_qlat": mla_q_lat_norm_g[None, :], "w_qup": w_qup,
        "g_kvlat": mla_kv_lat_norm_g[None, :], "w_k": w_k, "w_vt": w_vt.astype(BF16),
        "ones": jnp.asarray(ones.reshape(MLA_HEADS * VT_ROWS, 1)),
        "g_qm": jnp.pad(mla_q_norm_g, (0, pad_qk))[None, :] * (MLA_QK_DIM ** -0.5 * LOG2_E),
        "g_km": jnp.pad(mla_k_norm_g, (0, pad_qk))[None, :],
        "w_oa": w_oa.astype(BF16), "w_ob": w_out[SWA_WIDTH:].astype(BF16),
        "g_ffn": ffn_norm_g[None, :], "rw_hi": rw_hi, "rw_lo": rw_lo, "rb": rb,
    }


def _rope_tables(n_rows):
    half = MLA_ROPE_DIM // 2
    inv = 1.0 / (ROPE_THETA ** (jnp.arange(half, dtype=F32) / half))
    ang = jnp.arange(n_rows).astype(F32)[:, None] * inv[None, :]
    cos, sin = jnp.cos(ang), jnp.sin(ang)
    one = jnp.ones((n_rows, MLA_NOPE_DIM), F32)
    zero = jnp.zeros((n_rows, MLA_NOPE_DIM), F32)
    pad1 = jnp.ones((n_rows, LANES - MLA_QK_DIM), F32)
    pad0 = jnp.zeros((n_rows, LANES - MLA_QK_DIM), F32)
    return (jnp.concatenate([one, cos, cos, pad1], axis=1),
            jnp.concatenate([zero, -sin, sin, pad0], axis=1))


def kernel(x, meta_tokens, attn_norm_g, w_in, swa_q_norm_g, swa_k_norm_g, swa_sink, mla_q_lat_norm_g, w_mla_q_up, mla_kv_lat_norm_g, w_mla_kv_up, mla_q_norm_g, mla_k_norm_g, w_out, ffn_norm_g, router_w, router_b, w_gate, b_gate, w_up, b_up, w_down, b_down):
    bsz, n, d = x.shape
    assert bsz == 1 and d == D_MODEL and n % TK_MLA == 0 and attn_norm_g.shape[0] == 1
    wts = _prepare_weights(attn_norm_g[0], w_in[0], swa_q_norm_g[0], swa_k_norm_g[0], mla_q_lat_norm_g[0],
                           w_mla_q_up[0], mla_kv_lat_norm_g[0], w_mla_kv_up[0], mla_q_norm_g[0],
                           mla_k_norm_g[0], w_out[0], ffn_norm_g[0], router_w[0], router_b[0])
    xr = x[0]
    cos_t, sin_t = _rope_tables(N_META + n)

    xm = jnp.pad(meta_tokens.astype(F32), ((0, LANES - N_META), (0, 0)))
    valid_m = (jnp.arange(LANES) < N_META).astype(F32)[None, :]
    _, ks_m, vs_m, _, km_m, vt_m = _proj_call(xm, valid_m, cos_t[:LANES], sin_t[:LANES], wts, LANES)
    qs, ks, vs, qm, km, vt = _proj_call(xr, jnp.ones((1, n), F32), cos_t[N_META:], sin_t[N_META:], wts, TM_PROJ)

    ya = _swa_call(swa_sink[0], qs, ks, vs, ks_m, vs_m)
    yt = _mla_call(qm, km, vt, km_m, vt_m, TQ_MLA, TK_MLA)
    h1, m, ids, rank, gate, cnt = _post_call(xr, ya, yt, wts, TM_POST)

    te = TE_FFN
    counts = cnt[0, :N_EXPERTS].astype(jnp.int32)
    padded = (counts + te - 1) // te * te
    pad_end = jnp.cumsum(padded)
    pad_start = pad_end - padded
    expert = jnp.arange(N_EXPERTS, dtype=jnp.int32)[:, None, None]
    start_of = jnp.sum(jnp.where(ids[None, :TOP_K] == expert, pad_start[:, None, None], 0), axis=0)
    dest = (start_of + rank[:TOP_K]).reshape(-1)
    n_blocks = (n * TOP_K + N_EXPERTS * (te - 1)) // te
    block_start = jnp.arange(n_blocks, dtype=jnp.int32) * te
    block_e = jnp.minimum(jnp.sum((pad_end[None, :] <= block_start[:, None]).astype(jnp.int32), axis=1),
                          N_EXPERTS - 1)
    n_used = (pad_end[-1:] // te).astype(jnp.int32)
    tail_start = jnp.where(padded > 0, pad_end - te, -1).astype(jnp.int32)

    xs = _dispatch_call(dest, tail_start, n_used, m, n_blocks * te, TD_DISPATCH, te)
    ys = _ffn_call(block_e, n_used, xs, w_gate[0], b_gate[0], w_up[0], b_up[0], w_down[0], b_down[0], te)
    out = _combine_call(dest, h1, gate, ys, TC_COMBINE)
    return out[None]
```

```python
import functools

import numpy as np
import jax
import jax.numpy as jnp
from jax import lax
from jax.experimental import pallas as pl
from jax.experimental.pallas import tpu as pltpu

F32 = jnp.float32
BF16 = jnp.bfloat16

D_MODEL = 1024
N_META = 16
BLOCK = 128
WINDOW = 128
NORM_EPS = 1e-6

SWA_HEADS = 8
SWA_KV_HEADS = 2
SWA_GROUP = SWA_HEADS // SWA_KV_HEADS
SWA_HEAD_DIM = 64
SWA_WIDTH = SWA_HEADS * SWA_HEAD_DIM
SWA_KV_WIDTH = SWA_KV_HEADS * SWA_HEAD_DIM

MLA_HEADS = 8
MLA_Q_LORA = 256
MLA_KV_LORA = 128
MLA_NOPE_DIM = 64
MLA_ROPE_DIM = 32
MLA_V_DIM = 64
MLA_QK_DIM = MLA_NOPE_DIM + MLA_ROPE_DIM
MLA_WIDTH = MLA_HEADS * MLA_V_DIM
ROPE_THETA = 10000.0

N_EXPERTS = 32
TOP_K = 4
D_FF = 1024
SWIGLU_LIMIT = 7.0
SWIGLU_ALPHA = 1.702

LANES = 128
VT_ROWS = 80
NEG_BIG = -1e30
LOG2_E = 1.4426950408889634
VMEM_LIMIT = 48 * 1024 * 1024

TM_PROJ = 256
TQ_MLA = 512
TK_MLA = 512
MLA_UNROLL = 8
ROW_DMA_UNROLL = 8
TM_POST = 256
TD_DISPATCH = 128
TE_FFN = 256
TC_COMBINE = 128

_NT = (((1,), (1,)), ((), ()))


def _rms(v, n):
    return v * lax.rsqrt(jnp.sum(v * v, axis=-1, keepdims=True) * (1.0 / n) + NORM_EPS)


TILE_ROWS = D_MODEL // LANES


def _store_token_tiles(ref, v):
    rows = v.shape[0]
    for s in range(TILE_ROWS):
        ref[pl.ds(s, rows, stride=TILE_ROWS), :] = v[:, s * LANES:(s + 1) * LANES]


def _load_token_tiles(ref, rows):
    return jnp.concatenate([ref[pl.ds(s, rows, stride=TILE_ROWS), :] for s in range(TILE_ROWS)], axis=1)


def _proj_kernel(x_ref, valid_ref, cos_ref, sin_ref, g_attn_ref, w_in_ref, g_qs_ref, g_ks_ref,
                 g_qlat_ref, w_qup_ref, g_kvlat_ref, w_k_ref, w_vt_ref, ones_ref, g_qm_ref, g_km_ref,
                 qs_ref, ks_ref, vs_ref, qm_ref, km_ref, vt_ref):
    tm = x_ref.shape[0]
    a = _rms(x_ref[...], D_MODEL) * g_attn_ref[...]
    proj = jnp.dot(a.astype(BF16), w_in_ref[...], preferred_element_type=F32)

    lane = lax.broadcasted_iota(jnp.int32, (tm, LANES), 1)
    lo = lane < SWA_HEAD_DIM

    def seg_norm(v, g):
        v2 = v * v
        s_all = jnp.sum(v2, axis=-1, keepdims=True)
        s_lo = jnp.sum(jnp.where(lo, v2, 0.0), axis=-1, keepdims=True)
        ms = jnp.where(lo, s_lo, s_all - s_lo) * (1.0 / SWA_HEAD_DIM)
        return v * lax.rsqrt(ms + NORM_EPS) * g

    g_qs = g_qs_ref[...]
    for j in range(SWA_WIDTH // LANES):
        sl = slice(j * LANES, (j + 1) * LANES)
        qs_ref[:, sl] = seg_norm(proj[:, sl], g_qs).astype(BF16)
    o_k = SWA_WIDTH
    o_v = o_k + SWA_KV_WIDTH
    o_ql = o_v + SWA_KV_WIDTH
    o_kv = o_ql + MLA_Q_LORA
    o_kr = o_kv + MLA_KV_LORA
    ks_ref[...] = seg_norm(proj[:, o_k:o_v], g_ks_ref[...]).astype(BF16)
    vs_ref[...] = proj[:, o_v:o_ql].astype(BF16)

    cosv = cos_ref[...]
    sinv = sin_ref[...]
    first_half = lane < (MLA_NOPE_DIM + MLA_ROPE_DIM // 2)

    def norm_rope(v, g):
        ms = jnp.sum(v * v, axis=-1, keepdims=True) * (1.0 / MLA_QK_DIM)
        vn = v * lax.rsqrt(ms + NORM_EPS) * g
        rot = jnp.where(first_half,
                        pltpu.roll(vn, LANES - MLA_ROPE_DIM // 2, 1),
                        pltpu.roll(vn, MLA_ROPE_DIM // 2, 1))
        return vn * cosv + rot * sinv

    qln = _rms(proj[:, o_ql:o_kv], MLA_Q_LORA) * g_qlat_ref[...]
    qup = jnp.dot(qln.astype(BF16), w_qup_ref[...], preferred_element_type=F32)
    g_qm = g_qm_ref[...]
    for h in range(MLA_HEADS):
        qm_ref[h] = norm_rope(qup[:, h * LANES:(h + 1) * LANES], g_qm).astype(BF16)

    kvn = (_rms(proj[:, o_kv:o_kr], MLA_KV_LORA) * g_kvlat_ref[...]).astype(BF16)
    knope = jnp.dot(kvn, w_k_ref[...], preferred_element_type=F32)
    krope = proj[:, o_kr:o_kr + LANES]
    g_km = g_km_ref[...]
    for h in range(MLA_HEADS):
        km_ref[h] = norm_rope(knope[:, h * LANES:(h + 1) * LANES] + krope, g_km).astype(BF16)

    vt = lax.dot_general(w_vt_ref[...], kvn, _NT, preferred_element_type=F32)
    vt = vt + ones_ref[...] * valid_ref[...]
    for h in range(MLA_HEADS):
        vt_ref[h] = vt[h * VT_ROWS:(h + 1) * VT_ROWS].astype(BF16)


def _proj_call(x, valid, cos_t, sin_t, wts, tm):
    n = x.shape[0]
    w_names = ("g_attn", "w_in", "g_qs", "g_ks", "g_qlat", "w_qup", "g_kvlat", "w_k", "w_vt", "ones", "g_qm", "g_km")
    w_args = [wts[k] for k in w_names]

    def full(a):
        return pl.BlockSpec(a.shape, lambda i: (0,) * a.ndim)

    in_specs = [pl.BlockSpec((tm, D_MODEL), lambda i: (i, 0)),
                pl.BlockSpec((1, tm), lambda i: (0, i)),
                pl.BlockSpec((tm, LANES), lambda i: (i, 0)),
                pl.BlockSpec((tm, LANES), lambda i: (i, 0))] + [full(a) for a in w_args]
    out_shape = (jax.ShapeDtypeStruct((n, SWA_WIDTH), BF16),
                 jax.ShapeDtypeStruct((n, SWA_KV_WIDTH), BF16),
                 jax.ShapeDtypeStruct((n, SWA_KV_WIDTH), BF16),
                 jax.ShapeDtypeStruct((MLA_HEADS, n, LANES), BF16),
                 jax.ShapeDtypeStruct((MLA_HEADS, n, LANES), BF16),
                 jax.ShapeDtypeStruct((MLA_HEADS, VT_ROWS, n), BF16))
    out_specs = (pl.BlockSpec((tm, SWA_WIDTH), lambda i: (i, 0)),
                 pl.BlockSpec((tm, SWA_KV_WIDTH), lambda i: (i, 0)),
                 pl.BlockSpec((tm, SWA_KV_WIDTH), lambda i: (i, 0)),
                 pl.BlockSpec((MLA_HEADS, tm, LANES), lambda i: (0, i, 0)),
                 pl.BlockSpec((MLA_HEADS, tm, LANES), lambda i: (0, i, 0)),
                 pl.BlockSpec((MLA_HEADS, VT_ROWS, tm), lambda i: (0, 0, i)))
    return pl.pallas_call(
        _proj_kernel, grid=(n // tm,), in_specs=in_specs, out_specs=out_specs, out_shape=out_shape,
        compiler_params=pltpu.CompilerParams(dimension_semantics=("arbitrary",), vmem_limit_bytes=VMEM_LIMIT),
        name="proj",
    )(x, valid, cos_t, sin_t, *w_args)


def _swa_kernel(sink_ref, q_ref, kp_ref, kc_ref, kn_ref, vp_ref, vc_ref, vn_ref, km_ref, vm_ref, o_ref, *, slopes):
    b = pl.program_id(0)
    nb = pl.num_programs(0)
    nkeys = 3 * BLOCK + LANES
    kall = jnp.concatenate([kp_ref[...], kc_ref[...], kn_ref[...], km_ref[...]], axis=0)
    vall = jnp.concatenate([vp_ref[...], vc_ref[...], vn_ref[...], vm_ref[...]], axis=0)

    r = lax.broadcasted_iota(jnp.int32, (BLOCK, nkeys), 0)
    c = lax.broadcasted_iota(jnp.int32, (BLOCK, nkeys), 1)
    d = jnp.abs(BLOCK + r - c)
    lo_edge = jnp.where(b > 0, 0, BLOCK)
    hi_edge = jnp.where(b < nb - 1, 3 * BLOCK, 2 * BLOCK)
    in_win = (d <= WINDOW) & (c >= lo_edge) & (c < hi_edge)
    is_meta = (c >= 3 * BLOCK) & (c < 3 * BLOCK + N_META)
    valid = in_win | is_meta
    negdist = jnp.where(c < 3 * BLOCK, -d.astype(F32), 0.0)

    lane = lax.broadcasted_iota(jnp.int32, (BLOCK, LANES), 1)
    lo = lane < SWA_HEAD_DIM
    for j in range(SWA_GROUP):
        qg = q_ref[:, j * LANES:(j + 1) * LANES].astype(F32)
        q2 = jnp.concatenate([jnp.where(lo, qg, 0.0), jnp.where(lo, 0.0, qg)], axis=0).astype(BF16)
        s = lax.dot_general(q2, kall, _NT, preferred_element_type=F32)
        outs = []
        for half, h in ((0, j), (1, j + SWA_GROUP)):
            sh = s[half * BLOCK:(half + 1) * BLOCK]
            sh = jnp.where(valid, sh + slopes[h] * negdist, NEG_BIG)
            sink = sink_ref[h]
            m = jnp.maximum(jnp.max(sh, axis=-1, keepdims=True), sink)
            p = jnp.exp(sh - m)
            den = jnp.sum(p, axis=-1, keepdims=True) + jnp.exp(sink - m)
            o = jnp.dot(p.astype(BF16), vall, preferred_element_type=F32)
            outs.append(o / den)
        o_ref[:, j * LANES:(j + 1) * LANES] = jnp.where(lo, outs[0], outs[1]).astype(BF16)


def _swa_call(sink, qs, ks, vs, ks_meta, vs_meta):
    n = qs.shape[0]
    nb = n // BLOCK
    slopes = tuple(float(v) for v in 2.0 ** (-8.0 * np.arange(1, SWA_HEADS + 1) / SWA_HEADS))
    kv_prev = pl.BlockSpec((BLOCK, SWA_KV_WIDTH), lambda b: (jnp.maximum(b - 1, 0), 0))
    kv_cur = pl.BlockSpec((BLOCK, SWA_KV_WIDTH), lambda b: (b, 0))
    kv_next = pl.BlockSpec((BLOCK, SWA_KV_WIDTH), lambda b: (jnp.minimum(b + 1, nb - 1), 0))
    meta = pl.BlockSpec((LANES, SWA_KV_WIDTH), lambda b: (0, 0))
    return pl.pallas_call(
        functools.partial(_swa_kernel, slopes=slopes),
        grid=(nb,),
        in_specs=[pl.BlockSpec(memory_space=pltpu.SMEM),
                  pl.BlockSpec((BLOCK, SWA_WIDTH), lambda b: (b, 0)),
                  kv_prev, kv_cur, kv_next, kv_prev, kv_cur, kv_next, meta, meta],
        out_specs=pl.BlockSpec((BLOCK, SWA_WIDTH), lambda b: (b, 0)),
        out_shape=jax.ShapeDtypeStruct((n, SWA_WIDTH), BF16),
        compiler_params=pltpu.CompilerParams(dimension_semantics=("arbitrary",), vmem_limit_bytes=VMEM_LIMIT),
        name="swa",
    )(sink, qs, ks, ks, ks, vs, vs, vs, ks_meta, vs_meta)


def _mla_kernel(q_ref, k_ref, vt_ref, km_ref, vtm_ref, o_ref, s_scr, *, tk, nk, unroll):
    q = q_ref[0]
    tq = q.shape[0]

    def scores(off):
        return lax.dot_general(k_ref[0, pl.ds(off, tk), :], q, _NT, preferred_element_type=F32)

    def soft_pv(s, vt, m, acc):
        m_new = jnp.maximum(m, jnp.max(s, axis=0, keepdims=True))
        alpha = jnp.exp2(m - m_new)
        p = jnp.exp2(s - m_new).astype(BF16)
        acc = alpha * acc + jnp.dot(vt, p, preferred_element_type=F32)
        return m_new, acc

    m0 = jnp.full((1, tq), NEG_BIG, F32)
    acc0 = jnp.zeros((VT_ROWS, tq), F32)
    s_meta = lax.dot_general(km_ref[0], q, _NT, preferred_element_type=F32)
    s_scr[0] = scores(0)
    m, acc = soft_pv(s_meta, vtm_ref[0], m0, acc0)

    def body(i, carry):
        m, acc = carry
        for u in range(unroll):
            t = i * unroll + u
            cur = u % 2
            s_next = scores(pl.multiple_of(jnp.minimum(t + 1, nk - 1) * tk, tk))
            off = pl.multiple_of(t * tk, tk)
            m, acc = soft_pv(s_scr[cur], vt_ref[0, :, pl.ds(off, tk)], m, acc)
            s_scr[1 - cur] = s_next
        return m, acc

    m, acc = lax.fori_loop(0, nk // unroll, body, (m, acc))
    o_ref[...] = (acc[0:MLA_V_DIM] / acc[MLA_V_DIM:MLA_V_DIM + 1]).astype(BF16)


def _mla_call(qm, km, vt, km_meta, vt_meta, tq, tk):
    n = qm.shape[1]
    nk = n // tk
    unroll = MLA_UNROLL if nk % MLA_UNROLL == 0 else 2
    assert nk % unroll == 0
    return pl.pallas_call(
        functools.partial(_mla_kernel, tk=tk, nk=nk, unroll=unroll),
        grid=(MLA_HEADS, n // tq),
        in_specs=[pl.BlockSpec((1, tq, LANES), lambda h, i: (h, i, 0)),
                  pl.BlockSpec((1, n, LANES), lambda h, i: (h, 0, 0)),
                  pl.BlockSpec((1, VT_ROWS, n), lambda h, i: (h, 0, 0)),
                  pl.BlockSpec((1, LANES, LANES), lambda h, i: (h, 0, 0)),
                  pl.BlockSpec((1, VT_ROWS, LANES), lambda h, i: (h, 0, 0))],
        out_specs=pl.BlockSpec((MLA_V_DIM, tq), lambda h, i: (h, i)),
        out_shape=jax.ShapeDtypeStruct((MLA_WIDTH, n), BF16),
        scratch_shapes=[pltpu.VMEM((2, tk, tq), F32)],
        compiler_params=pltpu.CompilerParams(dimension_semantics=("arbitrary", "arbitrary"),
                                             vmem_limit_bytes=VMEM_LIMIT),
        name="mla",
    )(qm, km, vt, km_meta, vt_meta)


def _post_kernel(x_ref, ya_ref, yt_ref, w_oa_ref, w_ob_ref, g_ffn_ref, rw_hi_ref, rw_lo_ref, rb_ref,
                 h1_ref, m_ref, ids_ref, rank_ref, gate_ref, cnt_ref, base_ref):
    tm = x_ref.shape[0]

    @pl.when(pl.program_id(0) == 0)
    def _():
        base_ref[...] = jnp.zeros_like(base_ref)

    yb = yt_ref[...].T
    h1 = (x_ref[...]
          + jnp.dot(ya_ref[...], w_oa_ref[...], preferred_element_type=F32)
          + jnp.dot(yb, w_ob_ref[...], preferred_element_type=F32))
    h1_ref[...] = h1
    m = _rms(h1, D_MODEL) * g_ffn_ref[...]
    _store_token_tiles(m_ref, m)

    m_hi = m.astype(BF16)
    m_lo = (m - m_hi.astype(F32)).astype(BF16)
    rw_hi = rw_hi_ref[...]
    logits = (jnp.dot(m_hi, rw_hi, preferred_element_type=F32)
              + jnp.dot(m_lo, rw_hi, preferred_element_type=F32)
              + jnp.dot(m_hi, rw_lo_ref[...], preferred_element_type=F32)
              + rb_ref[...])

    lane = lax.broadcasted_iota(jnp.int32, (tm, LANES), 1)
    lane_f = lane.astype(F32)
    work = logits
    vals, idxs = [], []
    for _k in range(TOP_K):
        mx = jnp.max(work, axis=-1, keepdims=True)
        ix = jnp.min(jnp.where(work == mx, lane_f, float(LANES)), axis=-1, keepdims=True)
        vals.append(mx)
        idxs.append(ix)
        work = jnp.where(lane_f == ix, -3e38, work)
    exps = [jnp.exp(v - vals[0]) for v in vals]
    den = exps[0] + exps[1] + exps[2] + exps[3]

    hits = [lane_f == ix for ix in idxs]
    onehot = jnp.zeros((tm, LANES), F32)
    for hk in hits:
        onehot = onehot + jnp.where(hk, 1.0, 0.0)
    row = lax.broadcasted_iota(jnp.int32, (tm, tm), 0)
    col = lax.broadcasted_iota(jnp.int32, (tm, tm), 1)
    tri = jnp.where(row > col, 1.0, 0.0).astype(BF16)
    before = jnp.dot(tri, onehot.astype(BF16), preferred_element_type=F32) + base_ref[...]
    ids = jnp.zeros((tm, LANES), F32)
    rank = jnp.zeros((tm, LANES), F32)
    gate = jnp.zeros((tm, LANES), F32)
    for k in range(TOP_K):
        rk = jnp.sum(jnp.where(hits[k], before, 0.0), axis=-1, keepdims=True)
        sel = lane == k
        ids = jnp.where(sel, idxs[k], ids)
        rank = jnp.where(sel, rk, rank)
        gate = jnp.where(sel, exps[k] / den, gate)
    ids_ref[...] = ids.T[0:8].astype(jnp.int32)
    rank_ref[...] = rank.T[0:8].astype(jnp.int32)
    gate_ref[...] = gate
    base = base_ref[...] + jnp.sum(onehot, axis=0, keepdims=True)
    base_ref[...] = base
    cnt_ref[...] = base


def _post_call(x, ya, yt, wts, tm):
    n = x.shape[0]
    w_names = ("w_oa", "w_ob", "g_ffn", "rw_hi", "rw_lo", "rb")
    w_args = [wts[k] for k in w_names]

    def full(a):
        return pl.BlockSpec(a.shape, lambda i: (0,) * a.ndim)

    row = lambda w: pl.BlockSpec((tm, w), lambda i: (i, 0))
    choice = pl.BlockSpec((8, tm), lambda i: (0, i))
    return pl.pallas_call(
        _post_kernel, grid=(n // tm,),
        in_specs=[row(D_MODEL), row(SWA_WIDTH), pl.BlockSpec((MLA_WIDTH, tm), lambda i: (0, i))]
        + [full(a) for a in w_args],
        out_specs=(row(D_MODEL), pl.BlockSpec((tm * TILE_ROWS, LANES), lambda i: (i, 0)), choice, choice, row(LANES),
                   pl.BlockSpec((1, LANES), lambda i: (0, 0))),
        out_shape=(jax.ShapeDtypeStruct((n, D_MODEL), F32), jax.ShapeDtypeStruct((n * TILE_ROWS, LANES), F32),
                   jax.ShapeDtypeStruct((8, n), jnp.int32), jax.ShapeDtypeStruct((8, n), jnp.int32),
                   jax.ShapeDtypeStruct((n, LANES), F32), jax.ShapeDtypeStruct((1, LANES), F32)),
        scratch_shapes=[pltpu.VMEM((1, LANES), F32)],
        compiler_params=pltpu.CompilerParams(dimension_semantics=("arbitrary",), vmem_limit_bytes=VMEM_LIMIT),
        name="post",
    )(x, ya, yt, *w_args)


def _token_copy(src, src_tok, dst, dst_tok, sem):
    rows = lambda tok: pl.ds(pl.multiple_of(tok * TILE_ROWS, TILE_ROWS), TILE_ROWS)
    return pltpu.make_async_copy(src.at[rows(src_tok), :], dst.at[rows(dst_tok), :], sem)


def _dispatch_kernel(dest_ref, tail_ref, nused_ref, m_ref, xs_ref, zbuf, sem, zsem, *, td, n, te):
    block_rows = te * TILE_ROWS

    def zero_copy(slot):
        start = pl.multiple_of(slot * TILE_ROWS, block_rows)
        return pltpu.make_async_copy(zbuf, xs_ref.at[pl.ds(start, block_rows), :], zsem)

    @pl.when(pl.program_id(0) == 0)
    def _():
        zbuf[...] = jnp.zeros_like(zbuf)
        n_blocks = xs_ref.shape[0] // block_rows
        for e in range(N_EXPERTS):
            @pl.when(tail_ref[e] >= 0)
            def _():
                zero_copy(jnp.maximum(tail_ref[e], 0)).start()

        def start_unused(b, carry):
            zero_copy(b * te).start()
            return carry

        lax.fori_loop(nused_ref[0], n_blocks, start_unused, 0)
        for e in range(N_EXPERTS):
            @pl.when(tail_ref[e] >= 0)
            def _():
                zero_copy(0).wait()

        def wait_unused(b, carry):
            zero_copy(0).wait()
            return carry

        lax.fori_loop(nused_ref[0], n_blocks, wait_unused, 0)

    base = pl.program_id(0) * td

    def issue(tb, carry):
        for u in range(ROW_DMA_UNROLL):
            t = tb * ROW_DMA_UNROLL + u
            for k in range(TOP_K):
                _token_copy(m_ref, t, xs_ref, dest_ref[k * n + base + t], sem).start(priority=k % 2)
        return carry

    lax.fori_loop(0, td // ROW_DMA_UNROLL, issue, 0)

    def drain(tb, carry):
        for _u in range(ROW_DMA_UNROLL * TOP_K):
            _token_copy(m_ref, 0, xs_ref, 0, sem).wait()
        return carry

    lax.fori_loop(0, td // ROW_DMA_UNROLL, drain, 0)


def _dispatch_call(dest_flat, tail_start, n_used, m, n_slots, td, te):
    n = m.shape[0] // TILE_ROWS
    grid_spec = pltpu.PrefetchScalarGridSpec(
        num_scalar_prefetch=3, grid=(n // td,),
        in_specs=[pl.BlockSpec((td * TILE_ROWS, LANES), lambda i, d, z, u: (i, 0))],
        out_specs=pl.BlockSpec(memory_space=pl.ANY),
        scratch_shapes=[pltpu.VMEM((te * TILE_ROWS, LANES), F32), pltpu.SemaphoreType.DMA,
                        pltpu.SemaphoreType.DMA])
    return pl.pallas_call(
        functools.partial(_dispatch_kernel, td=td, n=n, te=te), grid_spec=grid_spec,
        out_shape=jax.ShapeDtypeStruct((n_slots * TILE_ROWS, LANES), F32),
        compiler_params=pltpu.CompilerParams(dimension_semantics=("arbitrary",), vmem_limit_bytes=VMEM_LIMIT),
        name="dispatch",
    )(dest_flat, tail_start, n_used, m)


def _ffn_kernel(be_ref, nused_ref, xs_ref, wg_ref, bg_ref, wu_ref, bu_ref, wd_ref, bd_ref, ys_ref,
                wg_s, wu_s, wd_s, *, te):
    b = pl.program_id(0)
    prev = be_ref[jnp.maximum(b - 1, 0)]
    fresh = (b == 0) | (be_ref[b] != prev)

    @pl.when(fresh & (b < nused_ref[0]))
    def _():
        wg_s[...] = wg_ref[0].astype(BF16)
        wu_s[...] = wu_ref[0].astype(BF16)
        wd_s[...] = wd_ref[0].astype(BF16)

    @pl.when(b < nused_ref[0])
    def _():
        x = _load_token_tiles(xs_ref, te).astype(BF16)
        g = jnp.dot(x, wg_s[...], preferred_element_type=F32) + bg_ref[0]
        u = jnp.dot(x, wu_s[...], preferred_element_type=F32) + bu_ref[0]
        g = jnp.minimum(g, SWIGLU_LIMIT)
        u = jnp.clip(u, -SWIGLU_LIMIT, SWIGLU_LIMIT)
        act = (u + 1.0) * (g * jax.nn.sigmoid(SWIGLU_ALPHA * g))
        y = jnp.dot(act.astype(BF16), wd_s[...], preferred_element_type=F32) + bd_ref[0]
        _store_token_tiles(ys_ref, y)

    @pl.when(b >= nused_ref[0])
    def _():
        ys_ref[...] = jnp.zeros_like(ys_ref)


def _ffn_call(block_e, n_used, xs, w_gate, b_gate, w_up, b_up, w_down, b_down, te):
    n_slots = xs.shape[0] // TILE_ROWS
    wspec = lambda r, c: pl.BlockSpec((1, r, c), lambda b, be, nu: (be[b], 0, 0))
    grid_spec = pltpu.PrefetchScalarGridSpec(
        num_scalar_prefetch=2, grid=(n_slots // te,),
        in_specs=[pl.BlockSpec((te * TILE_ROWS, LANES), lambda b, be, nu: (jnp.minimum(b, nu[0] - 1), 0)),
                  wspec(D_MODEL, D_FF), wspec(1, D_FF), wspec(D_MODEL, D_FF), wspec(1, D_FF),
                  wspec(D_FF, D_MODEL), wspec(1, D_MODEL)],
        out_specs=pl.BlockSpec((te * TILE_ROWS, LANES), lambda b, be, nu: (b, 0)),
        scratch_shapes=[pltpu.VMEM((D_MODEL, D_FF), BF16), pltpu.VMEM((D_MODEL, D_FF), BF16),
                        pltpu.VMEM((D_FF, D_MODEL), BF16)])
    return pl.pallas_call(
        functools.partial(_ffn_kernel, te=te), grid_spec=grid_spec,
        out_shape=jax.ShapeDtypeStruct((n_slots * TILE_ROWS, LANES), F32),
        compiler_params=pltpu.CompilerParams(dimension_semantics=("arbitrary",), vmem_limit_bytes=VMEM_LIMIT),
        name="ffn",
    )(block_e, n_used, xs, w_gate, b_gate[:, None, :], w_up, b_up[:, None, :], w_down, b_down[:, None, :])


def _combine_kernel(dest_ref, h1_ref, gate_ref, ys_ref, o_ref, buf, sem, *, tc, n):
    base = pl.program_id(0) * tc

    def issue(tb, carry):
        for u in range(ROW_DMA_UNROLL):
            t = tb * ROW_DMA_UNROLL + u
            for k in range(TOP_K):
                _token_copy(ys_ref, dest_ref[k * n + base + t], buf.at[k], t, sem).start(priority=k % 2)
        return carry

    lax.fori_loop(0, tc // ROW_DMA_UNROLL, issue, 0)

    def drain(tb, carry):
        for _u in range(ROW_DMA_UNROLL):
            for k in range(TOP_K):
                _token_copy(ys_ref, 0, buf.at[k], 0, sem).wait()
        return carry

    lax.fori_loop(0, tc // ROW_DMA_UNROLL, drain, 0)
    gate = gate_ref[...]
    out = h1_ref[...]
    for k in range(TOP_K):
        out = out + gate[:, k:k + 1] * _load_token_tiles(buf.at[k], tc)
    o_ref[...] = out


def _combine_call(dest_flat, h1, gate, ys, tc):
    n = h1.shape[0]
    grid_spec = pltpu.PrefetchScalarGridSpec(
        num_scalar_prefetch=1, grid=(n // tc,),
        in_specs=[pl.BlockSpec((tc, D_MODEL), lambda i, d: (i, 0)),
                  pl.BlockSpec((tc, LANES), lambda i, d: (i, 0)),
                  pl.BlockSpec(memory_space=pl.ANY)],
        out_specs=pl.BlockSpec((tc, D_MODEL), lambda i, d: (i, 0)),
        scratch_shapes=[pltpu.VMEM((TOP_K, tc * TILE_ROWS, LANES), F32), pltpu.SemaphoreType.DMA])
    return pl.pallas_call(
        functools.partial(_combine_kernel, tc=tc, n=n), grid_spec=grid_spec,
        out_shape=jax.ShapeDtypeStruct((n, D_MODEL), F32),
        compiler_params=pltpu.CompilerParams(dimension_semantics=("arbitrary",), vmem_limit_bytes=VMEM_LIMIT),
        name="combine",
    )(dest_flat, h1, gate, ys)


def _prepare_weights(attn_norm_g, w_in, swa_q_norm_g, swa_k_norm_g, mla_q_lat_norm_g, w_mla_q_up,
                     mla_kv_lat_norm_g, w_mla_kv_up, mla_q_norm_g, mla_k_norm_g, w_out, ffn_norm_g,
                     router_w, router_b):
    o_k = SWA_WIDTH
    o_v = o_k + SWA_KV_WIDTH
    o_ql = o_v + SWA_KV_WIDTH
    o_kv = o_ql + MLA_Q_LORA
    o_kr = o_kv + MLA_KV_LORA
    perm = np.arange(SWA_HEADS).reshape(SWA_KV_HEADS, SWA_GROUP).T.reshape(-1)
    w_qs = w_in[:, :o_k].reshape(D_MODEL, SWA_HEADS, SWA_HEAD_DIM)[:, perm].reshape(D_MODEL, SWA_WIDTH)
    w_kr = jnp.zeros((D_MODEL, LANES), F32).at[:, MLA_NOPE_DIM:MLA_QK_DIM].set(w_in[:, o_kr:])
    w_in_r = jnp.concatenate([w_qs, w_in[:, o_k:o_kr], w_kr], axis=1).astype(BF16)

    pad_qk = LANES - MLA_QK_DIM
    w_qup = jnp.pad(w_mla_q_up.reshape(MLA_Q_LORA, MLA_HEADS, MLA_QK_DIM), ((0, 0), (0, 0), (0, pad_qk)))
    w_qup = w_qup.reshape(MLA_Q_LORA, MLA_HEADS * LANES).astype(BF16)
    kv_up = w_mla_kv_up.reshape(MLA_KV_LORA, MLA_HEADS, MLA_NOPE_DIM + MLA_V_DIM)
    w_k = jnp.pad(kv_up[:, :, :MLA_NOPE_DIM], ((0, 0), (0, 0), (0, LANES - MLA_NOPE_DIM)))
    w_k = w_k.reshape(MLA_KV_LORA, MLA_HEADS * LANES).astype(BF16)
    w_vt = jnp.transpose(kv_up[:, :, MLA_NOPE_DIM:], (1, 2, 0))
    w_vt = jnp.pad(w_vt, ((0, 0), (0, VT_ROWS - MLA_V_DIM), (0, 0))).reshape(MLA_HEADS * VT_ROWS, MLA_KV_LORA)
    ones = np.zeros((MLA_HEADS, VT_ROWS, 1), np.float32)
    ones[:, MLA_V_DIM, 0] = 1.0

    w_oa = w_out[:SWA_WIDTH].reshape(SWA_HEADS, SWA_HEAD_DIM, D_MODEL)[perm].reshape(SWA_WIDTH, D_MODEL)
    rw = jnp.pad(router_w, ((0, 0), (0, LANES - N_EXPERTS)))
    rw_hi = rw.astype(BF16)
    rw_lo = (rw - rw_hi.astype(F32)).astype(BF16)
    rb = jnp.concatenate([router_b.astype(F32), jnp.full((LANES - N_EXPERTS,), NEG_BIG, F32)])[None, :]
    return {
        "g_attn": attn_norm_g[None, :], "w_in": w_in_r,
        "g_qs": jnp.tile(swa_q_norm_g, 2)[None, :] * (SWA_HEAD_DIM ** -0.5),
        "g_ks": jnp.tile(swa_k_norm_g, 2)[None, :],
        "g_qlat": mla_q_lat_norm_g[None, :], "w_qup": w_qup,
        "g_kvlat": mla_kv_lat_norm_g[None, :], "w_k": w_k, "w_vt": w_vt.astype(BF16),
        "ones": jnp.asarray(ones.reshape(MLA_HEADS * VT_ROWS, 1)),
        "g_qm": jnp.pad(mla_q_norm_g, (0, pad_qk))[None, :] * (MLA_QK_DIM ** -0.5 * LOG2_E),
        "g_km": jnp.pad(mla_k_norm_g, (0, pad_qk))[None, :],
        "w_oa": w_oa.astype(BF16), "w_ob": w_out[SWA_WIDTH:].astype(BF16),
        "g_ffn": ffn_norm_g[None, :], "rw_hi": rw_hi, "rw_lo": rw_lo, "rb": rb,
    }


def _rope_tables(n_rows):
    half = MLA_ROPE_DIM // 2
    inv = 1.0 / (ROPE_THETA ** (jnp.arange(half, dtype=F32) / half))
    ang = jnp.arange(n_rows).astype(F32)[:, None] * inv[None, :]
    cos, sin = jnp.cos(ang), jnp.sin(ang)
    one = jnp.ones((n_rows, MLA_NOPE_DIM), F32)
    zero = jnp.zeros((n_rows, MLA_NOPE_DIM), F32)
    pad1 = jnp.ones((n_rows, LANES - MLA_QK_DIM), F32)
    pad0 = jnp.zeros((n_rows, LANES - MLA_QK_DIM), F32)
    return (jnp.concatenate([one, cos, cos, pad1], axis=1),
            jnp.concatenate([zero, -sin, sin, pad0], axis=1))


def kernel(x, meta_tokens, attn_norm_g, w_in, swa_q_norm_g, swa_k_norm_g, swa_sink, mla_q_lat_norm_g, w_mla_q_up, mla_kv_lat_norm_g, w_mla_kv_up, mla_q_norm_g, mla_k_norm_g, w_out, ffn_norm_g, router_w, router_b, w_gate, b_gate, w_up, b_up, w_down, b_down):
    bsz, n, d = x.shape
    assert bsz == 1 and d == D_MODEL and n % TK_MLA == 0 and attn_norm_g.shape[0] == 1
    wts = _prepare_weights(attn_norm_g[0], w_in[0], swa_q_norm_g[0], swa_k_norm_g[0], mla_q_lat_norm_g[0],
                           w_mla_q_up[0], mla_kv_lat_norm_g[0], w_mla_kv_up[0], mla_q_norm_g[0],
                           mla_k_norm_g[0], w_out[0], ffn_norm_g[0], router_w[0], router_b[0])
    xr = x[0]
    cos_t, sin_t = _rope_tables(N_META + n)

    xm = jnp.pad(meta_tokens.astype(F32), ((0, LANES - N_META), (0, 0)))
    valid_m = (jnp.arange(LANES) < N_META).astype(F32)[None, :]
    _, ks_m, vs_m, _, km_m, vt_m = _proj_call(xm, valid_m, cos_t[:LANES], sin_t[:LANES], wts, LANES)
    qs, ks, vs, qm, km, vt = _proj_call(xr, jnp.ones((1, n), F32), cos_t[N_META:], sin_t[N_META:], wts, TM_PROJ)

    ya = _swa_call(swa_sink[0], qs, ks, vs, ks_m, vs_m)
    yt = _mla_call(qm, km, vt, km_m, vt_m, TQ_MLA, TK_MLA)
    h1, m, ids, rank, gate, cnt = _post_call(xr, ya, yt, wts, TM_POST)

    te = TE_FFN
    counts = cnt[0, :N_EXPERTS].astype(jnp.int32)
    padded = (counts + te - 1) // te * te
    pad_end = jnp.cumsum(padded)
    pad_start = pad_end - padded
    expert = jnp.arange(N_EXPERTS, dtype=jnp.int32)[:, None, None]
    start_of = jnp.sum(jnp.where(ids[None, :TOP_K] == expert, pad_start[:, None, None], 0), axis=0)
    dest = (start_of + rank[:TOP_K]).reshape(-1)
    n_blocks = (n * TOP_K + N_EXPERTS * (te - 1)) // te
    block_start = jnp.arange(n_blocks, dtype=jnp.int32) * te
    block_e = jnp.minimum(jnp.sum((pad_end[None, :] <= block_start[:, None]).astype(jnp.int32), axis=1),
                          N_EXPERTS - 1)
    n_used = (pad_end[-1:] // te).astype(jnp.int32)
    tail_start = jnp.where(padded > 0, pad_end - te, -1).astype(jnp.int32)

    xs = _dispatch_call(dest, tail_start, n_used, m, n_blocks * te, TD_DISPATCH, te)
    ys = _ffn_call(block_e, n_used, xs, w_gate[0], b_gate[0], w_up[0], b_up[0], w_down[0], b_down[0], te)
    out = _combine_call(dest, h1, gate, ys, TC_COMBINE)
    return out[None]
```

```python
import functools

import numpy as np
import jax
import jax.numpy as jnp
from jax import lax
from jax.experimental import pallas as pl
from jax.experimental.pallas import tpu as pltpu

F32 = jnp.float32
BF16 = jnp.bfloat16

D_MODEL = 1024
N_META = 16
BLOCK = 128
WINDOW = 128
NORM_EPS = 1e-6

SWA_HEADS = 8
SWA_KV_HEADS = 2
SWA_GROUP = SWA_HEADS // SWA_KV_HEADS
SWA_HEAD_DIM = 64
SWA_WIDTH = SWA_HEADS * SWA_HEAD_DIM
SWA_KV_WIDTH = SWA_KV_HEADS * SWA_HEAD_DIM

MLA_HEADS = 8
MLA_Q_LORA = 256
MLA_KV_LORA = 128
MLA_NOPE_DIM = 64
MLA_ROPE_DIM = 32
MLA_V_DIM = 64
MLA_QK_DIM = MLA_NOPE_DIM + MLA_ROPE_DIM
MLA_WIDTH = MLA_HEADS * MLA_V_DIM
ROPE_THETA = 10000.0

N_EXPERTS = 32
TOP_K = 4
D_FF = 1024
SWIGLU_LIMIT = 7.0
SWIGLU_ALPHA = 1.702

LANES = 128
VT_ROWS = 80
NEG_BIG = -1e30
LOG2_E = 1.4426950408889634
VMEM_LIMIT = 48 * 1024 * 1024

TM_PROJ = 256
TQ_MLA = 512
TK_MLA = 512
MLA_UNROLL = 16
ROW_DMA_UNROLL = 8
TM_POST = 256
TD_DISPATCH = 128
TE_FFN = 256
TC_COMBINE = 128

_NT = (((1,), (1,)), ((), ()))


def _rms(v, n):
    return v * lax.rsqrt(jnp.sum(v * v, axis=-1, keepdims=True) * (1.0 / n) + NORM_EPS)


TILE_ROWS = D_MODEL // LANES


def _store_token_tiles(ref, v):
    rows = v.shape[0]
    for s in range(TILE_ROWS):
        ref[pl.ds(s, rows, stride=TILE_ROWS), :] = v[:, s * LANES:(s + 1) * LANES]


def _load_token_tiles(ref, rows):
    return jnp.concatenate([ref[pl.ds(s, rows, stride=TILE_ROWS), :] for s in range(TILE_ROWS)], axis=1)


def _proj_kernel(x_ref, valid_ref, cos_ref, sin_ref, g_attn_ref, w_in_ref, g_qs_ref, g_ks_ref,
                 g_qlat_ref, w_qup_ref, g_kvlat_ref, w_k_ref, w_vt_ref, ones_ref, g_qm_ref, g_km_ref,
                 g_qm_rot_ref, g_km_rot_ref,
                 qs_ref, ks_ref, vs_ref, qm_ref, km_ref, vt_ref):
    tm = x_ref.shape[0]
    a = _rms(x_ref[...], D_MODEL) * g_attn_ref[...]
    proj = jnp.dot(a.astype(BF16), w_in_ref[...], preferred_element_type=F32)

    lane = lax.broadcasted_iota(jnp.int32, (tm, LANES), 1)
    lo = lane < SWA_HEAD_DIM

    def seg_norm(v, g):
        v2 = v * v
        s_all = jnp.sum(v2, axis=-1, keepdims=True)
        s_lo = jnp.sum(jnp.where(lo, v2, 0.0), axis=-1, keepdims=True)
        ms = jnp.where(lo, s_lo, s_all - s_lo) * (1.0 / SWA_HEAD_DIM)
        return v * lax.rsqrt(ms + NORM_EPS) * g

    g_qs = g_qs_ref[...]
    for j in range(SWA_WIDTH // LANES):
        sl = slice(j * LANES, (j + 1) * LANES)
        qs_ref[:, sl] = seg_norm(proj[:, sl], g_qs).astype(BF16)
    o_k = SWA_WIDTH
    o_v = o_k + SWA_KV_WIDTH
    o_ql = o_v + SWA_KV_WIDTH
    o_kv = o_ql + MLA_Q_LORA
    o_kr = o_kv + MLA_KV_LORA
    ks_ref[...] = seg_norm(proj[:, o_k:o_v], g_ks_ref[...]).astype(BF16)
    vs_ref[...] = proj[:, o_v:o_ql].astype(BF16)

    cosv = cos_ref[...]
    sinv = sin_ref[...]

    def norm_rope(v, v_rot, g, g_rot):
        ms = jnp.sum(v * v, axis=-1, keepdims=True) * (1.0 / MLA_QK_DIM)
        return (v * g * cosv + v_rot * g_rot * sinv) * lax.rsqrt(ms + NORM_EPS)

    hw = MLA_HEADS * LANES
    qln = _rms(proj[:, o_ql:o_kv], MLA_Q_LORA) * g_qlat_ref[...]
    qup = jnp.dot(qln.astype(BF16), w_qup_ref[...], preferred_element_type=F32)
    g_qm = g_qm_ref[...]
    g_qm_rot = g_qm_rot_ref[...]
    for h in range(MLA_HEADS):
        sl = slice(h * LANES, (h + 1) * LANES)
        sl_rot = slice(hw + h * LANES, hw + (h + 1) * LANES)
        qm_ref[h] = norm_rope(qup[:, sl], qup[:, sl_rot], g_qm, g_qm_rot).astype(BF16)

    kvn = (_rms(proj[:, o_kv:o_kr], MLA_KV_LORA) * g_kvlat_ref[...]).astype(BF16)
    knope = jnp.dot(kvn, w_k_ref[...], preferred_element_type=F32)
    krope = proj[:, o_kr:o_kr + LANES]
    krope_rot = proj[:, o_kr + LANES:o_kr + 2 * LANES]
    g_km = g_km_ref[...]
    g_km_rot = g_km_rot_ref[...]
    for h in range(MLA_HEADS):
        km_ref[h] = norm_rope(knope[:, h * LANES:(h + 1) * LANES] + krope, krope_rot, g_km, g_km_rot).astype(BF16)

    vt = lax.dot_general(w_vt_ref[...], kvn, _NT, preferred_element_type=F32)
    vt = vt + ones_ref[...] * valid_ref[...]
    for h in range(MLA_HEADS):
        vt_ref[h] = vt[h * VT_ROWS:(h + 1) * VT_ROWS].astype(BF16)


def _proj_call(x, valid, cos_t, sin_t, wts, tm):
    n = x.shape[0]
    w_names = ("g_attn", "w_in", "g_qs", "g_ks", "g_qlat", "w_qup", "g_kvlat", "w_k", "w_vt", "ones", "g_qm", "g_km",
               "g_qm_rot", "g_km_rot")
    w_args = [wts[k] for k in w_names]

    def full(a):
        return pl.BlockSpec(a.shape, lambda i: (0,) * a.ndim)

    in_specs = [pl.BlockSpec((tm, D_MODEL), lambda i: (i, 0)),
                pl.BlockSpec((1, tm), lambda i: (0, i)),
                pl.BlockSpec((tm, LANES), lambda i: (i, 0)),
                pl.BlockSpec((tm, LANES), lambda i: (i, 0))] + [full(a) for a in w_args]
    out_shape = (jax.ShapeDtypeStruct((n, SWA_WIDTH), BF16),
                 jax.ShapeDtypeStruct((n, SWA_KV_WIDTH), BF16),
                 jax.ShapeDtypeStruct((n, SWA_KV_WIDTH), BF16),
                 jax.ShapeDtypeStruct((MLA_HEADS, n, LANES), BF16),
                 jax.ShapeDtypeStruct((MLA_HEADS, n, LANES), BF16),
                 jax.ShapeDtypeStruct((MLA_HEADS, VT_ROWS, n), BF16))
    out_specs = (pl.BlockSpec((tm, SWA_WIDTH), lambda i: (i, 0)),
                 pl.BlockSpec((tm, SWA_KV_WIDTH), lambda i: (i, 0)),
                 pl.BlockSpec((tm, SWA_KV_WIDTH), lambda i: (i, 0)),
                 pl.BlockSpec((MLA_HEADS, tm, LANES), lambda i: (0, i, 0)),
                 pl.BlockSpec((MLA_HEADS, tm, LANES), lambda i: (0, i, 0)),
                 pl.BlockSpec((MLA_HEADS, VT_ROWS, tm), lambda i: (0, 0, i)))
    return pl.pallas_call(
        _proj_kernel, grid=(n // tm,), in_specs=in_specs, out_specs=out_specs, out_shape=out_shape,
        compiler_params=pltpu.CompilerParams(dimension_semantics=("arbitrary",), vmem_limit_bytes=VMEM_LIMIT),
        name="proj",
    )(x, valid, cos_t, sin_t, *w_args)


def _swa_bias_table():
    nkeys = 3 * BLOCK + LANES
    r = np.arange(BLOCK)[:, None]
    c = np.arange(nkeys)[None, :]
    d = np.abs(BLOCK + r - c)
    in_win = (d <= WINDOW) & (c < 3 * BLOCK)
    is_meta = (c >= 3 * BLOCK) & (c < 3 * BLOCK + N_META)
    slopes = 2.0 ** (-8.0 * np.arange(1, SWA_HEADS + 1) / SWA_HEADS)
    bias = np.where(in_win[None], -slopes[:, None, None] * d[None], np.where(is_meta[None], 0.0, NEG_BIG))
    return jnp.asarray(bias, F32)


def _swa_kernel(sink_ref, q_ref, kp_ref, kc_ref, kn_ref, vp_ref, vc_ref, vn_ref, km_ref, vm_ref, bias_ref, o_ref):
    b = pl.program_id(0)
    nb = pl.num_programs(0)
    nkeys = 3 * BLOCK + LANES
    kall = jnp.concatenate([kp_ref[...], kc_ref[...], kn_ref[...], km_ref[...]], axis=0)
    vall = jnp.concatenate([vp_ref[...], vc_ref[...], vn_ref[...], vm_ref[...]], axis=0)

    c = lax.broadcasted_iota(jnp.int32, (1, nkeys), 1)
    lo_edge = jnp.where(b > 0, 0, BLOCK)
    hi_edge = jnp.where(b < nb - 1, 3 * BLOCK, 2 * BLOCK)
    edge = jnp.where((c < lo_edge) | ((c >= hi_edge) & (c < 3 * BLOCK)), NEG_BIG, 0.0)

    lane = lax.broadcasted_iota(jnp.int32, (BLOCK, LANES), 1)
    lo = lane < SWA_HEAD_DIM
    for j in range(SWA_GROUP):
        qg = q_ref[:, j * LANES:(j + 1) * LANES].astype(F32)
        q2 = jnp.concatenate([jnp.where(lo, qg, 0.0), jnp.where(lo, 0.0, qg)], axis=0).astype(BF16)
        s = lax.dot_general(q2, kall, _NT, preferred_element_type=F32)
        outs = []
        for half, h in ((0, j), (1, j + SWA_GROUP)):
            sh = s[half * BLOCK:(half + 1) * BLOCK] + bias_ref[h] + edge
            sink = sink_ref[h]
            m = jnp.maximum(jnp.max(sh, axis=-1, keepdims=True), sink)
            p = jnp.exp(sh - m)
            den = jnp.sum(p, axis=-1, keepdims=True) + jnp.exp(sink - m)
            o = jnp.dot(p.astype(BF16), vall, preferred_element_type=F32)
            outs.append(o / den)
        o_ref[:, j * LANES:(j + 1) * LANES] = jnp.where(lo, outs[0], outs[1]).astype(BF16)


def _swa_call(sink, qs, ks, vs, ks_meta, vs_meta):
    n = qs.shape[0]
    nb = n // BLOCK
    bias = _swa_bias_table()
    kv_prev = pl.BlockSpec((BLOCK, SWA_KV_WIDTH), lambda b: (jnp.maximum(b - 1, 0), 0))
    kv_cur = pl.BlockSpec((BLOCK, SWA_KV_WIDTH), lambda b: (b, 0))
    kv_next = pl.BlockSpec((BLOCK, SWA_KV_WIDTH), lambda b: (jnp.minimum(b + 1, nb - 1), 0))
    meta = pl.BlockSpec((LANES, SWA_KV_WIDTH), lambda b: (0, 0))
    return pl.pallas_call(
        _swa_kernel,
        grid=(nb,),
        in_specs=[pl.BlockSpec(memory_space=pltpu.SMEM),
                  pl.BlockSpec((BLOCK, SWA_WIDTH), lambda b: (b, 0)),
                  kv_prev, kv_cur, kv_next, kv_prev, kv_cur, kv_next, meta, meta,
                  pl.BlockSpec(bias.shape, lambda b: (0, 0, 0))],
        out_specs=pl.BlockSpec((BLOCK, SWA_WIDTH), lambda b: (b, 0)),
        out_shape=jax.ShapeDtypeStruct((n, SWA_WIDTH), BF16),
        compiler_params=pltpu.CompilerParams(dimension_semantics=("arbitrary",), vmem_limit_bytes=VMEM_LIMIT),
        name="swa",
    )(sink, qs, ks, ks, ks, vs, vs, vs, ks_meta, vs_meta, bias)


def _mla_kernel(q_ref, k_ref, vt_ref, km_ref, vtm_ref, o_ref, s_scr, *, tk, nk, unroll):
    q = q_ref[0]
    tq = q.shape[0]

    def scores(off):
        return lax.dot_general(k_ref[0, pl.ds(off, tk), :], q, _NT, preferred_element_type=F32)

    def soft_pv(s, vt, m, acc):
        m_new = jnp.maximum(m, jnp.max(s, axis=0, keepdims=True))
        alpha = jnp.exp2(m - m_new)
        p = jnp.exp2(s - m_new).astype(BF16)
        acc = alpha * acc + jnp.dot(vt, p, preferred_element_type=F32)
        return m_new, acc

    m0 = jnp.full((1, tq), NEG_BIG, F32)
    acc0 = jnp.zeros((VT_ROWS, tq), F32)
    s_meta = lax.dot_general(km_ref[0], q, _NT, preferred_element_type=F32)
    s_scr[0] = scores(0)
    m, acc = soft_pv(s_meta, vtm_ref[0], m0, acc0)

    def body(i, carry):
        m, acc = carry
        for u in range(unroll):
            t = i * unroll + u
            cur = u % 2
            s_next = scores(pl.multiple_of(jnp.minimum(t + 1, nk - 1) * tk, tk))
            off = pl.multiple_of(t * tk, tk)
            m, acc = soft_pv(s_scr[cur], vt_ref[0, :, pl.ds(off, tk)], m, acc)
            s_scr[1 - cur] = s_next
        return m, acc

    m, acc = lax.fori_loop(0, nk // unroll, body, (m, acc))
    o_ref[...] = (acc[0:MLA_V_DIM] / acc[MLA_V_DIM:MLA_V_DIM + 1]).astype(BF16)


def _mla_call(qm, km, vt, km_meta, vt_meta, tq, tk):
    n = qm.shape[1]
    nk = n // tk
    unroll = MLA_UNROLL if nk % MLA_UNROLL == 0 else 2
    assert nk % unroll == 0
    return pl.pallas_call(
        functools.partial(_mla_kernel, tk=tk, nk=nk, unroll=unroll),
        grid=(MLA_HEADS, n // tq),
        in_specs=[pl.BlockSpec((1, tq, LANES), lambda h, i: (h, i, 0)),
                  pl.BlockSpec((1, n, LANES), lambda h, i: (h, 0, 0)),
                  pl.BlockSpec((1, VT_ROWS, n), lambda h, i: (h, 0, 0)),
                  pl.BlockSpec((1, LANES, LANES), lambda h, i: (h, 0, 0)),
                  pl.BlockSpec((1, VT_ROWS, LANES), lambda h, i: (h, 0, 0))],
        out_specs=pl.BlockSpec((MLA_V_DIM, tq), lambda h, i: (h, i)),
        out_shape=jax.ShapeDtypeStruct((MLA_WIDTH, n), BF16),
        scratch_shapes=[pltpu.VMEM((2, tk, tq), F32)],
        compiler_params=pltpu.CompilerParams(dimension_semantics=("arbitrary", "arbitrary"),
                                             vmem_limit_bytes=VMEM_LIMIT),
        name="mla",
    )(qm, km, vt, km_meta, vt_meta)


def _post_kernel(x_ref, ya_ref, yt_ref, w_oa_ref, w_ob_ref, g_ffn_ref, rw_hi_ref, rw_lo_ref, rb_ref,
                 h1_ref, m_ref, ids_ref, rank_ref, gate_ref, cnt_ref, base_ref):
    tm = x_ref.shape[0]

    @pl.when(pl.program_id(0) == 0)
    def _():
        base_ref[...] = jnp.zeros_like(base_ref)

    yb = yt_ref[...].T
    h1 = (x_ref[...]
          + jnp.dot(ya_ref[...], w_oa_ref[...], preferred_element_type=F32)
          + jnp.dot(yb, w_ob_ref[...], preferred_element_type=F32))
    h1_ref[...] = h1
    m = _rms(h1, D_MODEL) * g_ffn_ref[...]
    _store_token_tiles(m_ref, m)

    m_hi = m.astype(BF16)
    m_lo = (m - m_hi.astype(F32)).astype(BF16)
    rw_hi = rw_hi_ref[...]
    logits = (jnp.dot(m_hi, rw_hi, preferred_element_type=F32)
              + jnp.dot(m_lo, rw_hi, preferred_element_type=F32)
              + jnp.dot(m_hi, rw_lo_ref[...], preferred_element_type=F32)
              + rb_ref[...])

    lane = lax.broadcasted_iota(jnp.int32, (tm, LANES), 1)
    lane_f = lane.astype(F32)
    work = logits
    vals, idxs = [], []
    for _k in range(TOP_K):
        mx = jnp.max(work, axis=-1, keepdims=True)
        ix = jnp.min(jnp.where(work == mx, lane_f, float(LANES)), axis=-1, keepdims=True)
        vals.append(mx)
        idxs.append(ix)
        work = jnp.where(lane_f == ix, -3e38, work)
    exps = [jnp.exp(v - vals[0]) for v in vals]
    den = exps[0] + exps[1] + exps[2] + exps[3]

    hits = [lane_f == ix for ix in idxs]
    onehot = jnp.zeros((tm, LANES), F32)
    for hk in hits:
        onehot = onehot + jnp.where(hk, 1.0, 0.0)
    row = lax.broadcasted_iota(jnp.int32, (tm, tm), 0)
    col = lax.broadcasted_iota(jnp.int32, (tm, tm), 1)
    tri = jnp.where(row > col, 1.0, 0.0).astype(BF16)
    before = jnp.dot(tri, onehot.astype(BF16), preferred_element_type=F32) + base_ref[...]
    ids = jnp.zeros((tm, LANES), F32)
    rank = jnp.zeros((tm, LANES), F32)
    gate = jnp.zeros((tm, LANES), F32)
    for k in range(TOP_K):
        rk = jnp.sum(jnp.where(hits[k], before, 0.0), axis=-1, keepdims=True)
        sel = lane == k
        ids = jnp.where(sel, idxs[k], ids)
        rank = jnp.where(sel, rk, rank)
        gate = jnp.where(sel, exps[k] / den, gate)
    ids_ref[...] = ids.T[0:8].astype(jnp.int32)
    rank_ref[...] = rank.T[0:8].astype(jnp.int32)
    gate_ref[...] = gate
    base = base_ref[...] + jnp.sum(onehot, axis=0, keepdims=True)
    base_ref[...] = base
    cnt_ref[...] = base


def _post_call(x, ya, yt, wts, tm):
    n = x.shape[0]
    w_names = ("w_oa", "w_ob", "g_ffn", "rw_hi", "rw_lo", "rb")
    w_args = [wts[k] for k in w_names]

    def full(a):
        return pl.BlockSpec(a.shape, lambda i: (0,) * a.ndim)

    row = lambda w: pl.BlockSpec((tm, w), lambda i: (i, 0))
    choice = pl.BlockSpec((8, tm), lambda i: (0, i))
    return pl.pallas_call(
        _post_kernel, grid=(n // tm,),
        in_specs=[row(D_MODEL), row(SWA_WIDTH), pl.BlockSpec((MLA_WIDTH, tm), lambda i: (0, i))]
        + [full(a) for a in w_args],
        out_specs=(row(D_MODEL), pl.BlockSpec((tm * TILE_ROWS, LANES), lambda i: (i, 0)), choice, choice, row(LANES),
                   pl.BlockSpec((1, LANES), lambda i: (0, 0))),
        out_shape=(jax.ShapeDtypeStruct((n, D_MODEL), F32), jax.ShapeDtypeStruct((n * TILE_ROWS, LANES), F32),
                   jax.ShapeDtypeStruct((8, n), jnp.int32), jax.ShapeDtypeStruct((8, n), jnp.int32),
                   jax.ShapeDtypeStruct((n, LANES), F32), jax.ShapeDtypeStruct((1, LANES), F32)),
        scratch_shapes=[pltpu.VMEM((1, LANES), F32)],
        compiler_params=pltpu.CompilerParams(dimension_semantics=("arbitrary",), vmem_limit_bytes=VMEM_LIMIT),
        name="post",
    )(x, ya, yt, *w_args)


def _token_copy(src, src_tok, dst, dst_tok, sem):
    rows = lambda tok: pl.ds(pl.multiple_of(tok * TILE_ROWS, TILE_ROWS), TILE_ROWS)
    return pltpu.make_async_copy(src.at[rows(src_tok), :], dst.at[rows(dst_tok), :], sem)


def _dispatch_kernel(dest_ref, tail_ref, nused_ref, m_ref, xs_ref, zbuf, sem, zsem, *, td, n, te):
    block_rows = te * TILE_ROWS

    def zero_copy(slot):
        start = pl.multiple_of(slot * TILE_ROWS, block_rows)
        return pltpu.make_async_copy(zbuf, xs_ref.at[pl.ds(start, block_rows), :], zsem)

    @pl.when(pl.program_id(0) == 0)
    def _():
        zbuf[...] = jnp.zeros_like(zbuf)
        n_blocks = xs_ref.shape[0] // block_rows
        for e in range(N_EXPERTS):
            @pl.when(tail_ref[e] >= 0)
            def _():
                zero_copy(jnp.maximum(tail_ref[e], 0)).start()

        def start_unused(b, carry):
            zero_copy(b * te).start()
            return carry

        lax.fori_loop(nused_ref[0], n_blocks, start_unused, 0)
        for e in range(N_EXPERTS):
            @pl.when(tail_ref[e] >= 0)
            def _():
                zero_copy(0).wait()

        def wait_unused(b, carry):
            zero_copy(0).wait()
            return carry

        lax.fori_loop(nused_ref[0], n_blocks, wait_unused, 0)

    base = pl.program_id(0) * td

    def issue(tb, carry):
        for u in range(ROW_DMA_UNROLL):
            t = tb * ROW_DMA_UNROLL + u
            for k in range(TOP_K):
                _token_copy(m_ref, t, xs_ref, dest_ref[k * n + base + t], sem).start(priority=k % 2)
        return carry

    lax.fori_loop(0, td // ROW_DMA_UNROLL, issue, 0)

    def drain(tb, carry):
        for _u in range(ROW_DMA_UNROLL * TOP_K):
            _token_copy(m_ref, 0, xs_ref, 0, sem).wait()
        return carry

    lax.fori_loop(0, td // ROW_DMA_UNROLL, drain, 0)


def _dispatch_call(dest_flat, tail_start, n_used, m, n_slots, td, te):
    n = m.shape[0] // TILE_ROWS
    grid_spec = pltpu.PrefetchScalarGridSpec(
        num_scalar_prefetch=3, grid=(n // td,),
        in_specs=[pl.BlockSpec((td * TILE_ROWS, LANES), lambda i, d, z, u: (i, 0))],
        out_specs=pl.BlockSpec(memory_space=pl.ANY),
        scratch_shapes=[pltpu.VMEM((te * TILE_ROWS, LANES), F32), pltpu.SemaphoreType.DMA,
                        pltpu.SemaphoreType.DMA])
    return pl.pallas_call(
        functools.partial(_dispatch_kernel, td=td, n=n, te=te), grid_spec=grid_spec,
        out_shape=jax.ShapeDtypeStruct((n_slots * TILE_ROWS, LANES), F32),
        compiler_params=pltpu.CompilerParams(dimension_semantics=("arbitrary",), vmem_limit_bytes=VMEM_LIMIT),
        name="dispatch",
    )(dest_flat, tail_start, n_used, m)


def _ffn_kernel(be_ref, nused_ref, xs_ref, wg_ref, bg_ref, wu_ref, bu_ref, wd_ref, bd_ref, ys_ref,
                wg_s, wu_s, wd_s, *, te):
    b = pl.program_id(0)
    prev = be_ref[jnp.maximum(b - 1, 0)]
    fresh = (b == 0) | (be_ref[b] != prev)

    @pl.when(fresh & (b < nused_ref[0]))
    def _():
        wg_s[...] = wg_ref[0].astype(BF16)
        wu_s[...] = wu_ref[0].astype(BF16)
        wd_s[...] = wd_ref[0].astype(BF16)

    @pl.when(b < nused_ref[0])
    def _():
        x = _load_token_tiles(xs_ref, te).astype(BF16)
        g = jnp.dot(x, wg_s[...], preferred_element_type=F32) + bg_ref[0]
        u = jnp.dot(x, wu_s[...], preferred_element_type=F32) + bu_ref[0]
        g = jnp.minimum(g, SWIGLU_LIMIT)
        u = jnp.clip(u, -SWIGLU_LIMIT, SWIGLU_LIMIT)
        act = (u + 1.0) * (g * jax.nn.sigmoid(SWIGLU_ALPHA * g))
        y = jnp.dot(act.astype(BF16), wd_s[...], preferred_element_type=F32) + bd_ref[0]
        _store_token_tiles(ys_ref, y)

    @pl.when(b >= nused_ref[0])
    def _():
        ys_ref[...] = jnp.zeros_like(ys_ref)


def _ffn_call(block_e, n_used, xs, w_gate, b_gate, w_up, b_up, w_down, b_down, te):
    n_slots = xs.shape[0] // TILE_ROWS
    wspec = lambda r, c: pl.BlockSpec((1, r, c), lambda b, be, nu: (be[b], 0, 0))
    grid_spec = pltpu.PrefetchScalarGridSpec(
        num_scalar_prefetch=2, grid=(n_slots // te,),
        in_specs=[pl.BlockSpec((te * TILE_ROWS, LANES), lambda b, be, nu: (jnp.minimum(b, nu[0] - 1), 0)),
                  wspec(D_MODEL, D_FF), wspec(1, D_FF), wspec(D_MODEL, D_FF), wspec(1, D_FF),
                  wspec(D_FF, D_MODEL), wspec(1, D_MODEL)],
        out_specs=pl.BlockSpec((te * TILE_ROWS, LANES), lambda b, be, nu: (b, 0)),
        scratch_shapes=[pltpu.VMEM((D_MODEL, D_FF), BF16), pltpu.VMEM((D_MODEL, D_FF), BF16),
                        pltpu.VMEM((D_FF, D_MODEL), BF16)])
    return pl.pallas_call(
        functools.partial(_ffn_kernel, te=te), grid_spec=grid_spec,
        out_shape=jax.ShapeDtypeStruct((n_slots * TILE_ROWS, LANES), F32),
        compiler_params=pltpu.CompilerParams(dimension_semantics=("arbitrary",), vmem_limit_bytes=VMEM_LIMIT),
        name="ffn",
    )(block_e, n_used, xs, w_gate, b_gate[:, None, :], w_up, b_up[:, None, :], w_down, b_down[:, None, :])


def _combine_kernel(dest_ref, h1_ref, gate_ref, ys_ref, o_ref, buf, sem, *, tc, n):
    i = pl.program_id(0)

    def gather(step, slot):
        base = step * tc

        def issue(tb, carry):
            for u in range(ROW_DMA_UNROLL):
                t = tb * ROW_DMA_UNROLL + u
                for k in range(TOP_K):
                    _token_copy(ys_ref, dest_ref[k * n + base + t], buf.at[slot, k], t,
                                sem.at[slot]).start(priority=k % 2)
            return carry

        lax.fori_loop(0, tc // ROW_DMA_UNROLL, issue, 0)

    @pl.when(i == 0)
    def _():
        gather(0, 0)

    slot = i % 2

    @pl.when(i + 1 < pl.num_programs(0))
    def _():
        gather(i + 1, 1 - slot)

    def drain(tb, carry):
        for _u in range(ROW_DMA_UNROLL):
            for k in range(TOP_K):
                _token_copy(ys_ref, 0, buf.at[slot, k], 0, sem.at[slot]).wait()
        return carry

    lax.fori_loop(0, tc // ROW_DMA_UNROLL, drain, 0)
    gate = gate_ref[...]
    out = h1_ref[...]
    for k in range(TOP_K):
        out = out + gate[:, k:k + 1] * _load_token_tiles(buf.at[slot, k], tc)
    o_ref[...] = out


def _combine_call(dest_flat, h1, gate, ys, tc):
    n = h1.shape[0]
    grid_spec = pltpu.PrefetchScalarGridSpec(
        num_scalar_prefetch=1, grid=(n // tc,),
        in_specs=[pl.BlockSpec((tc, D_MODEL), lambda i, d: (i, 0)),
                  pl.BlockSpec((tc, LANES), lambda i, d: (i, 0)),
                  pl.BlockSpec(memory_space=pl.ANY)],
        out_specs=pl.BlockSpec((tc, D_MODEL), lambda i, d: (i, 0)),
        scratch_shapes=[pltpu.VMEM((2, TOP_K, tc * TILE_ROWS, LANES), F32), pltpu.SemaphoreType.DMA((2,))])
    return pl.pallas_call(
        functools.partial(_combine_kernel, tc=tc, n=n), grid_spec=grid_spec,
        out_shape=jax.ShapeDtypeStruct((n, D_MODEL), F32),
        compiler_params=pltpu.CompilerParams(dimension_semantics=("arbitrary",), vmem_limit_bytes=VMEM_LIMIT),
        name="combine",
    )(dest_flat, h1, gate, ys)


def _prepare_weights(attn_norm_g, w_in, swa_q_norm_g, swa_k_norm_g, mla_q_lat_norm_g, w_mla_q_up,
                     mla_kv_lat_norm_g, w_mla_kv_up, mla_q_norm_g, mla_k_norm_g, w_out, ffn_norm_g,
                     router_w, router_b):
    o_k = SWA_WIDTH
    o_v = o_k + SWA_KV_WIDTH
    o_ql = o_v + SWA_KV_WIDTH
    o_kv = o_ql + MLA_Q_LORA
    o_kr = o_kv + MLA_KV_LORA
    perm = np.arange(SWA_HEADS).reshape(SWA_KV_HEADS, SWA_GROUP).T.reshape(-1)
    w_qs = w_in[:, :o_k].reshape(D_MODEL, SWA_HEADS, SWA_HEAD_DIM)[:, perm].reshape(D_MODEL, SWA_WIDTH)
    w_kr = jnp.zeros((D_MODEL, LANES), F32).at[:, MLA_NOPE_DIM:MLA_QK_DIM].set(w_in[:, o_kr:])
    half = MLA_ROPE_DIM // 2
    r0, r1, r2 = MLA_NOPE_DIM, MLA_NOPE_DIM + half, MLA_QK_DIM

    def partner_lanes(a):
        z = jnp.zeros_like(a)
        return z.at[..., r0:r1].set(a[..., r1:r2]).at[..., r1:r2].set(a[..., r0:r1])

    w_in_r = jnp.concatenate([w_qs, w_in[:, o_k:o_kr], w_kr, partner_lanes(w_kr)], axis=1).astype(BF16)

    pad_qk = LANES - MLA_QK_DIM
    w_qup = jnp.pad(w_mla_q_up.reshape(MLA_Q_LORA, MLA_HEADS, MLA_QK_DIM), ((0, 0), (0, 0), (0, pad_qk)))
    w_qup = jnp.concatenate([w_qup.reshape(MLA_Q_LORA, MLA_HEADS * LANES),
                             partner_lanes(w_qup).reshape(MLA_Q_LORA, MLA_HEADS * LANES)], axis=1).astype(BF16)
    kv_up = w_mla_kv_up.reshape(MLA_KV_LORA, MLA_HEADS, MLA_NOPE_DIM + MLA_V_DIM)
    w_k = jnp.pad(kv_up[:, :, :MLA_NOPE_DIM], ((0, 0), (0, 0), (0, LANES - MLA_NOPE_DIM)))
    w_k = w_k.reshape(MLA_KV_LORA, MLA_HEADS * LANES).astype(BF16)
    w_vt = jnp.transpose(kv_up[:, :, MLA_NOPE_DIM:], (1, 2, 0))
    w_vt = jnp.pad(w_vt, ((0, 0), (0, VT_ROWS - MLA_V_DIM), (0, 0))).reshape(MLA_HEADS * VT_ROWS, MLA_KV_LORA)
    ones = np.zeros((MLA_HEADS, VT_ROWS, 1), np.float32)
    ones[:, MLA_V_DIM, 0] = 1.0

    w_oa = w_out[:SWA_WIDTH].reshape(SWA_HEADS, SWA_HEAD_DIM, D_MODEL)[perm].reshape(SWA_WIDTH, D_MODEL)
    rw = jnp.pad(router_w, ((0, 0), (0, LANES - N_EXPERTS)))
    rw_hi = rw.astype(BF16)
    rw_lo = (rw - rw_hi.astype(F32)).astype(BF16)
    rb = jnp.concatenate([router_b.astype(F32), jnp.full((LANES - N_EXPERTS,), NEG_BIG, F32)])[None, :]
    return {
        "g_attn": attn_norm_g[None, :], "w_in": w_in_r,
        "g_qs": jnp.tile(swa_q_norm_g, 2)[None, :] * (SWA_HEAD_DIM ** -0.5),
        "g_ks": jnp.tile(swa_k_norm_g, 2)[None, :],
        "g_qlat": mla_q_lat_norm_g[None, :], "w_qup": w_qup,
        "g_kvlat": mla_kv_lat_norm_g[None, :], "w_k": w_k, "w_vt": w_vt.astype(BF16),
        "ones": jnp.asarray(ones.reshape(MLA_HEADS * VT_ROWS, 1)),
        "g_qm": jnp.pad(mla_q_norm_g, (0, pad_qk))[None, :] * (MLA_QK_DIM ** -0.5 * LOG2_E),
        "g_km": jnp.pad(mla_k_norm_g, (0, pad_qk))[None, :],
        "g_qm_rot": partner_lanes(jnp.pad(mla_q_norm_g, (0, pad_qk))[None, :] * (MLA_QK_DIM ** -0.5 * LOG2_E)),
        "g_km_rot": partner_lanes(jnp.pad(mla_k_norm_g, (0, pad_qk))[None, :]),
        "w_oa": w_oa.astype(BF16), "w_ob": w_out[SWA_WIDTH:].astype(BF16),
        "g_ffn": ffn_norm_g[None, :], "rw_hi": rw_hi, "rw_lo": rw_lo, "rb": rb,
    }


def _rope_tables(n_rows):
    half = MLA_ROPE_DIM // 2
    inv = 1.0 / (ROPE_THETA ** (jnp.arange(half, dtype=F32) / half))
    ang = jnp.arange(n_rows).astype(F32)[:, None] * inv[None, :]
    cos, sin = jnp.cos(ang), jnp.sin(ang)
    one = jnp.ones((n_rows, MLA_NOPE_DIM), F32)
    zero = jnp.zeros((n_rows, MLA_NOPE_DIM), F32)
    pad1 = jnp.ones((n_rows, LANES - MLA_QK_DIM), F32)
    pad0 = jnp.zeros((n_rows, LANES - MLA_QK_DIM), F32)
    return (jnp.concatenate([one, cos, cos, pad1], axis=1),
            jnp.concatenate([zero, -sin, sin, pad0], axis=1))


def kernel(x, meta_tokens, attn_norm_g, w_in, swa_q_norm_g, swa_k_norm_g, swa_sink, mla_q_lat_norm_g, w_mla_q_up, mla_kv_lat_norm_g, w_mla_kv_up, mla_q_norm_g, mla_k_norm_g, w_out, ffn_norm_g, router_w, router_b, w_gate, b_gate, w_up, b_up, w_down, b_down):
    bsz, n, d = x.shape
    assert bsz == 1 and d == D_MODEL and n % TK_MLA == 0 and attn_norm_g.shape[0] == 1
    wts = _prepare_weights(attn_norm_g[0], w_in[0], swa_q_norm_g[0], swa_k_norm_g[0], mla_q_lat_norm_g[0],
                           w_mla_q_up[0], mla_kv_lat_norm_g[0], w_mla_kv_up[0], mla_q_norm_g[0],
                           mla_k_norm_g[0], w_out[0], ffn_norm_g[0], router_w[0], router_b[0])
    xr = x[0]
    cos_t, sin_t = _rope_tables(N_META + n)

    xm = jnp.pad(meta_tokens.astype(F32), ((0, LANES - N_META), (0, 0)))
    valid_m = (jnp.arange(LANES) < N_META).astype(F32)[None, :]
    _, ks_m, vs_m, _, km_m, vt_m = _proj_call(xm, valid_m, cos_t[:LANES], sin_t[:LANES], wts, LANES)
    qs, ks, vs, qm, km, vt = _proj_call(xr, jnp.ones((1, n), F32), cos_t[N_META:], sin_t[N_META:], wts, TM_PROJ)

    ya = _swa_call(swa_sink[0], qs, ks, vs, ks_m, vs_m)
    yt = _mla_call(qm, km, vt, km_m, vt_m, TQ_MLA, TK_MLA)
    h1, m, ids, rank, gate, cnt = _post_call(xr, ya, yt, wts, TM_POST)

    te = TE_FFN
    counts = cnt[0, :N_EXPERTS].astype(jnp.int32)
    padded = (counts + te - 1) // te * te
    pad_end = jnp.cumsum(padded)
    pad_start = pad_end - padded
    expert = jnp.arange(N_EXPERTS, dtype=jnp.int32)[:, None, None]
    start_of = jnp.sum(jnp.where(ids[None, :TOP_K] == expert, pad_start[:, None, None], 0), axis=0)
    dest = (start_of + rank[:TOP_K]).reshape(-1)
    n_blocks = (n * TOP_K + N_EXPERTS * (te - 1)) // te
    block_start = jnp.arange(n_blocks, dtype=jnp.int32) * te
    block_e = jnp.minimum(jnp.sum((pad_end[None, :] <= block_start[:, None]).astype(jnp.int32), axis=1),
                          N_EXPERTS - 1)
    n_used = (pad_end[-1:] // te).astype(jnp.int32)
    tail_start = jnp.where(padded > 0, pad_end - te, -1).astype(jnp.int32)

    xs = _dispatch_call(dest, tail_start, n_used, m, n_blocks * te, TD_DISPATCH, te)
    ys = _ffn_call(block_e, n_used, xs, w_gate[0], b_gate[0], w_up[0], b_up[0], w_down[0], b_down[0], te)
    out = _combine_call(dest, h1, gate, ys, TC_COMBINE)
    return out[None]
```

```python
import functools

import numpy as np
import jax
import jax.numpy as jnp
from jax import lax
from jax.experimental import pallas as pl
from jax.experimental.pallas import tpu as pltpu

F32 = jnp.float32
BF16 = jnp.bfloat16

D_MODEL = 1024
N_META = 16
BLOCK = 128
WINDOW = 128
NORM_EPS = 1e-6

SWA_HEADS = 8
SWA_KV_HEADS = 2
SWA_GROUP = SWA_HEADS // SWA_KV_HEADS
SWA_HEAD_DIM = 64
SWA_WIDTH = SWA_HEADS * SWA_HEAD_DIM
SWA_KV_WIDTH = SWA_KV_HEADS * SWA_HEAD_DIM

MLA_HEADS = 8
MLA_Q_LORA = 256
MLA_KV_LORA = 128
MLA_NOPE_DIM = 64
MLA_ROPE_DIM = 32
MLA_V_DIM = 64
MLA_QK_DIM = MLA_NOPE_DIM + MLA_ROPE_DIM
MLA_WIDTH = MLA_HEADS * MLA_V_DIM
ROPE_THETA = 10000.0

N_EXPERTS = 32
TOP_K = 4
D_FF = 1024
SWIGLU_LIMIT = 7.0
SWIGLU_ALPHA = 1.702

LANES = 128
VT_ROWS = 80
NEG_BIG = -1e30
LOG2_E = 1.4426950408889634
VMEM_LIMIT = 48 * 1024 * 1024
FFN_VMEM_LIMIT = 56 * 1024 * 1024

TM_PROJ = 512
TQ_MLA = 512
TK_MLA = 512
MLA_UNROLL = 16
ROW_DMA_UNROLL = 8
TM_POST = 512
TD_DISPATCH = 128
TE_FFN = 256
TC_COMBINE = 128

_NT = (((1,), (1,)), ((), ()))


def _rms(v, n):
    return v * lax.rsqrt(jnp.sum(v * v, axis=-1, keepdims=True) * (1.0 / n) + NORM_EPS)


TILE_ROWS = D_MODEL // LANES


def _store_token_tiles(ref, v):
    rows = v.shape[0]
    for s in range(TILE_ROWS):
        ref[pl.ds(s, rows, stride=TILE_ROWS), :] = v[:, s * LANES:(s + 1) * LANES]


def _load_token_tiles(ref, rows):
    return jnp.concatenate([ref[pl.ds(s, rows, stride=TILE_ROWS), :] for s in range(TILE_ROWS)], axis=1)


def _proj_kernel(x_ref, valid_ref, cos_ref, sin_ref, g_attn_ref, w_in_ref, g_qs_ref, g_ks_ref,
                 g_qlat_ref, w_qup_ref, g_kvlat_ref, w_k_ref, w_vt_ref, ones_ref, g_qm_ref, g_km_ref,
                 g_qm_rot_ref, g_km_rot_ref,
                 qs_ref, ks_ref, vs_ref, qm_ref, km_ref, vt_ref):
    tm = x_ref.shape[0]
    a = _rms(x_ref[...], D_MODEL) * g_attn_ref[...]
    proj = jnp.dot(a.astype(BF16), w_in_ref[...], preferred_element_type=F32)

    lane = lax.broadcasted_iota(jnp.int32, (tm, LANES), 1)
    lo = lane < SWA_HEAD_DIM

    def seg_norm(v, g):
        v2 = v * v
        s_all = jnp.sum(v2, axis=-1, keepdims=True)
        s_lo = jnp.sum(jnp.where(lo, v2, 0.0), axis=-1, keepdims=True)
        ms = jnp.where(lo, s_lo, s_all - s_lo) * (1.0 / SWA_HEAD_DIM)
        return v * lax.rsqrt(ms + NORM_EPS) * g

    g_qs = g_qs_ref[...]
    for j in range(SWA_WIDTH // LANES):
        sl = slice(j * LANES, (j + 1) * LANES)
        qs_ref[:, sl] = seg_norm(proj[:, sl], g_qs).astype(BF16)
    o_k = SWA_WIDTH
    o_v = o_k + SWA_KV_WIDTH
    o_ql = o_v + SWA_KV_WIDTH
    o_kv = o_ql + MLA_Q_LORA
    o_kr = o_kv + MLA_KV_LORA
    ks_ref[...] = seg_norm(proj[:, o_k:o_v], g_ks_ref[...]).astype(BF16)
    vs_ref[...] = proj[:, o_v:o_ql].astype(BF16)

    cosv = cos_ref[...]
    sinv = sin_ref[...]

    def norm_rope(v, v_rot, g, g_rot):
        ms = jnp.sum(v * v, axis=-1, keepdims=True) * (1.0 / MLA_QK_DIM)
        return (v * g * cosv + v_rot * g_rot * sinv) * lax.rsqrt(ms + NORM_EPS)

    hw = MLA_HEADS * LANES
    qln = _rms(proj[:, o_ql:o_kv], MLA_Q_LORA) * g_qlat_ref[...]
    qup = jnp.dot(qln.astype(BF16), w_qup_ref[...], preferred_element_type=F32)
    g_qm = g_qm_ref[...]
    g_qm_rot = g_qm_rot_ref[...]
    for h in range(MLA_HEADS):
        sl = slice(h * LANES, (h + 1) * LANES)
        sl_rot = slice(hw + h * LANES, hw + (h + 1) * LANES)
        qm_ref[h] = norm_rope(qup[:, sl], qup[:, sl_rot], g_qm, g_qm_rot).T.astype(BF16)

    kvn = (_rms(proj[:, o_kv:o_kr], MLA_KV_LORA) * g_kvlat_ref[...]).astype(BF16)
    knope = jnp.dot(kvn, w_k_ref[...], preferred_element_type=F32)
    krope = proj[:, o_kr:o_kr + LANES]
    krope_rot = proj[:, o_kr + LANES:o_kr + 2 * LANES]
    g_km = g_km_ref[...]
    g_km_rot = g_km_rot_ref[...]
    for h in range(MLA_HEADS):
        km_ref[h] = norm_rope(knope[:, h * LANES:(h + 1) * LANES] + krope, krope_rot, g_km, g_km_rot).astype(BF16)

    vt = lax.dot_general(w_vt_ref[...], kvn, _NT, preferred_element_type=F32)
    vt = vt + ones_ref[...] * valid_ref[...]
    for h in range(MLA_HEADS):
        vt_ref[h] = vt[h * VT_ROWS:(h + 1) * VT_ROWS].astype(BF16)


def _proj_call(x, valid, cos_t, sin_t, wts, tm):
    n = x.shape[0]
    w_names = ("g_attn", "w_in", "g_qs", "g_ks", "g_qlat", "w_qup", "g_kvlat", "w_k", "w_vt", "ones", "g_qm", "g_km",
               "g_qm_rot", "g_km_rot")
    w_args = [wts[k] for k in w_names]

    def full(a):
        return pl.BlockSpec(a.shape, lambda i: (0,) * a.ndim)

    in_specs = [pl.BlockSpec((tm, D_MODEL), lambda i: (i, 0)),
                pl.BlockSpec((1, tm), lambda i: (0, i)),
                pl.BlockSpec((tm, LANES), lambda i: (i, 0)),
                pl.BlockSpec((tm, LANES), lambda i: (i, 0))] + [full(a) for a in w_args]
    out_shape = (jax.ShapeDtypeStruct((n, SWA_WIDTH), BF16),
                 jax.ShapeDtypeStruct((n, SWA_KV_WIDTH), BF16),
                 jax.ShapeDtypeStruct((n, SWA_KV_WIDTH), BF16),
                 jax.ShapeDtypeStruct((MLA_HEADS, LANES, n), BF16),
                 jax.ShapeDtypeStruct((MLA_HEADS, n, LANES), BF16),
                 jax.ShapeDtypeStruct((MLA_HEADS, VT_ROWS, n), BF16))
    out_specs = (pl.BlockSpec((tm, SWA_WIDTH), lambda i: (i, 0)),
                 pl.BlockSpec((tm, SWA_KV_WIDTH), lambda i: (i, 0)),
                 pl.BlockSpec((tm, SWA_KV_WIDTH), lambda i: (i, 0)),
                 pl.BlockSpec((MLA_HEADS, LANES, tm), lambda i: (0, 0, i)),
                 pl.BlockSpec((MLA_HEADS, tm, LANES), lambda i: (0, i, 0)),
                 pl.BlockSpec((MLA_HEADS, VT_ROWS, tm), lambda i: (0, 0, i)))
    return pl.pallas_call(
        _proj_kernel, grid=(n // tm,), in_specs=in_specs, out_specs=out_specs, out_shape=out_shape,
        compiler_params=pltpu.CompilerParams(dimension_semantics=("arbitrary",), vmem_limit_bytes=VMEM_LIMIT),
        name="proj",
    )(x, valid, cos_t, sin_t, *w_args)


def _swa_bias_table():
    nkeys = 3 * BLOCK + LANES
    r = np.arange(BLOCK)[:, None]
    c = np.arange(nkeys)[None, :]
    d = np.abs(BLOCK + r - c)
    in_win = (d <= WINDOW) & (c < 3 * BLOCK)
    is_meta = (c >= 3 * BLOCK) & (c < 3 * BLOCK + N_META)
    slopes = 2.0 ** (-8.0 * np.arange(1, SWA_HEADS + 1) / SWA_HEADS)
    bias = np.where(in_win[None], -slopes[:, None, None] * d[None], np.where(is_meta[None], 0.0, NEG_BIG))
    return jnp.asarray(bias, F32)


def _swa_kernel(sink_ref, q_ref, kp_ref, kc_ref, kn_ref, vp_ref, vc_ref, vn_ref, km_ref, vm_ref, bias_ref, o_ref):
    b = pl.program_id(0)
    nb = pl.num_programs(0)
    nkeys = 3 * BLOCK + LANES
    kall = jnp.concatenate([kp_ref[...], kc_ref[...], kn_ref[...], km_ref[...]], axis=0)
    vall = jnp.concatenate([vp_ref[...], vc_ref[...], vn_ref[...], vm_ref[...]], axis=0)

    c = lax.broadcasted_iota(jnp.int32, (1, nkeys), 1)
    lo_edge = jnp.where(b > 0, 0, BLOCK)
    hi_edge = jnp.where(b < nb - 1, 3 * BLOCK, 2 * BLOCK)
    edge = jnp.where((c < lo_edge) | ((c >= hi_edge) & (c < 3 * BLOCK)), NEG_BIG, 0.0)

    lane = lax.broadcasted_iota(jnp.int32, (BLOCK, LANES), 1)
    lo = lane < SWA_HEAD_DIM
    for j in range(SWA_GROUP):
        qg = q_ref[:, j * LANES:(j + 1) * LANES].astype(F32)
        q2 = jnp.concatenate([jnp.where(lo, qg, 0.0), jnp.where(lo, 0.0, qg)], axis=0).astype(BF16)
        s = lax.dot_general(q2, kall, _NT, preferred_element_type=F32)
        outs = []
        for half, h in ((0, j), (1, j + SWA_GROUP)):
            sh = s[half * BLOCK:(half + 1) * BLOCK] + bias_ref[h] + edge
            sink = sink_ref[h]
            m = jnp.maximum(jnp.max(sh, axis=-1, keepdims=True), sink)
            p = jnp.exp(sh - m)
            den = jnp.sum(p, axis=-1, keepdims=True) + jnp.exp(sink - m)
            o = jnp.dot(p.astype(BF16), vall, preferred_element_type=F32)
            outs.append(o / den)
        o_ref[:, j * LANES:(j + 1) * LANES] = jnp.where(lo, outs[0], outs[1]).astype(BF16)


def _swa_call(sink, qs, ks, vs, ks_meta, vs_meta):
    n = qs.shape[0]
    nb = n // BLOCK
    bias = _swa_bias_table()
    kv_prev = pl.BlockSpec((BLOCK, SWA_KV_WIDTH), lambda b: (jnp.maximum(b - 1, 0), 0))
    kv_cur = pl.BlockSpec((BLOCK, SWA_KV_WIDTH), lambda b: (b, 0))
    kv_next = pl.BlockSpec((BLOCK, SWA_KV_WIDTH), lambda b: (jnp.minimum(b + 1, nb - 1), 0))
    meta = pl.BlockSpec((LANES, SWA_KV_WIDTH), lambda b: (0, 0))
    return pl.pallas_call(
        _swa_kernel,
        grid=(nb,),
        in_specs=[pl.BlockSpec(memory_space=pltpu.SMEM),
                  pl.BlockSpec((BLOCK, SWA_WIDTH), lambda b: (b, 0)),
                  kv_prev, kv_cur, kv_next, kv_prev, kv_cur, kv_next, meta, meta,
                  pl.BlockSpec(bias.shape, lambda b: (0, 0, 0))],
        out_specs=pl.BlockSpec((BLOCK, SWA_WIDTH), lambda b: (b, 0)),
        out_shape=jax.ShapeDtypeStruct((n, SWA_WIDTH), BF16),
        compiler_params=pltpu.CompilerParams(dimension_semantics=("arbitrary",), vmem_limit_bytes=VMEM_LIMIT),
        name="swa",
    )(sink, qs, ks, ks, ks, vs, vs, vs, ks_meta, vs_meta, bias)


def _mla_kernel(q_ref, k_ref, vt_ref, km_ref, vtm_ref, o_ref, s_scr, *, tk, nk, unroll):
    qt = q_ref[0]
    tq = qt.shape[1]

    def scores(off):
        return jnp.dot(k_ref[0, pl.ds(off, tk), :], qt, preferred_element_type=F32)

    def soft_pv(s, vt, m, acc):
        m_new = jnp.maximum(m, jnp.max(s, axis=0, keepdims=True))
        alpha = jnp.exp2(m - m_new)
        p = jnp.exp2(s - m_new).astype(BF16)
        acc = alpha * acc + jnp.dot(vt, p, preferred_element_type=F32)
        return m_new, acc

    m0 = jnp.full((1, tq), NEG_BIG, F32)
    acc0 = jnp.zeros((VT_ROWS, tq), F32)
    s_meta = jnp.dot(km_ref[0], qt, preferred_element_type=F32)
    s_scr[0] = scores(0)
    m, acc = soft_pv(s_meta, vtm_ref[0], m0, acc0)

    def body(i, carry):
        m, acc = carry
        for u in range(unroll):
            t = i * unroll + u
            cur = u % 2
            s_next = scores(pl.multiple_of(jnp.minimum(t + 1, nk - 1) * tk, tk))
            off = pl.multiple_of(t * tk, tk)
            m, acc = soft_pv(s_scr[cur], vt_ref[0, :, pl.ds(off, tk)], m, acc)
            s_scr[1 - cur] = s_next
        return m, acc

    m, acc = lax.fori_loop(0, nk // unroll, body, (m, acc))
    o_ref[...] = (acc[0:MLA_V_DIM] / acc[MLA_V_DIM:MLA_V_DIM + 1]).astype(BF16)


def _mla_call(qm, km, vt, km_meta, vt_meta, tq, tk):
    n = qm.shape[2]
    nk = n // tk
    unroll = MLA_UNROLL if nk % MLA_UNROLL == 0 else 2
    assert nk % unroll == 0
    return pl.pallas_call(
        functools.partial(_mla_kernel, tk=tk, nk=nk, unroll=unroll),
        grid=(MLA_HEADS, n // tq),
        in_specs=[pl.BlockSpec((1, LANES, tq), lambda h, i: (h, 0, i)),
                  pl.BlockSpec((1, n, LANES), lambda h, i: (h, 0, 0)),
                  pl.BlockSpec((1, VT_ROWS, n), lambda h, i: (h, 0, 0)),
                  pl.BlockSpec((1, LANES, LANES), lambda h, i: (h, 0, 0)),
                  pl.BlockSpec((1, VT_ROWS, LANES), lambda h, i: (h, 0, 0))],
        out_specs=pl.BlockSpec((MLA_V_DIM, tq), lambda h, i: (h, i)),
        out_shape=jax.ShapeDtypeStruct((MLA_WIDTH, n), BF16),
        scratch_shapes=[pltpu.VMEM((2, tk, tq), F32)],
        compiler_params=pltpu.CompilerParams(dimension_semantics=("arbitrary", "arbitrary"),
                                             vmem_limit_bytes=VMEM_LIMIT),
        name="mla",
    )(qm, km, vt, km_meta, vt_meta)


def _post_kernel(x_ref, ya_ref, yt_ref, w_oa_ref, w_ob_ref, g_ffn_ref, rw_hi_ref, rw_lo_ref, rb_ref,
                 h1_ref, m_ref, ids_ref, rank_ref, gate_ref, cnt_ref, base_ref):
    tm = x_ref.shape[0]

    @pl.when(pl.program_id(0) == 0)
    def _():
        base_ref[...] = jnp.zeros_like(base_ref)

    yb = yt_ref[...].T
    h1 = (x_ref[...]
          + jnp.dot(ya_ref[...], w_oa_ref[...], preferred_element_type=F32)
          + jnp.dot(yb, w_ob_ref[...], preferred_element_type=F32))
    h1_ref[...] = h1
    m = _rms(h1, D_MODEL) * g_ffn_ref[...]
    _store_token_tiles(m_ref, m)

    m_hi = m.astype(BF16)
    m_lo = (m - m_hi.astype(F32)).astype(BF16)
    rw_hi = rw_hi_ref[...]
    logits = (jnp.dot(m_hi, rw_hi, preferred_element_type=F32)
              + jnp.dot(m_lo, rw_hi, preferred_element_type=F32)
              + jnp.dot(m_hi, rw_lo_ref[...], preferred_element_type=F32)
              + rb_ref[...])

    lane = lax.broadcasted_iota(jnp.int32, (tm, LANES), 1)
    lane_f = lane.astype(F32)
    work = logits
    vals, idxs = [], []
    for _k in range(TOP_K):
        mx = jnp.max(work, axis=-1, keepdims=True)
        ix = jnp.min(jnp.where(work == mx, lane_f, float(LANES)), axis=-1, keepdims=True)
        vals.append(mx)
        idxs.append(ix)
        work = jnp.where(lane_f == ix, -3e38, work)
    exps = [jnp.exp(v - vals[0]) for v in vals]
    den = exps[0] + exps[1] + exps[2] + exps[3]

    hits = [lane_f == ix for ix in idxs]
    onehot = jnp.zeros((tm, LANES), F32)
    for hk in hits:
        onehot = onehot + jnp.where(hk, 1.0, 0.0)
    row = lax.broadcasted_iota(jnp.int32, (tm, tm), 0)
    col = lax.broadcasted_iota(jnp.int32, (tm, tm), 1)
    tri = jnp.where(row > col, 1.0, 0.0).astype(BF16)
    before = jnp.dot(tri, onehot.astype(BF16), preferred_element_type=F32) + base_ref[...]
    ids = jnp.zeros((tm, LANES), F32)
    rank = jnp.zeros((tm, LANES), F32)
    gate = jnp.zeros((tm, LANES), F32)
    for k in range(TOP_K):
        rk = jnp.sum(jnp.where(hits[k], before, 0.0), axis=-1, keepdims=True)
        sel = lane == k
        ids = jnp.where(sel, idxs[k], ids)
        rank = jnp.where(sel, rk, rank)
        gate = jnp.where(sel, exps[k] / den, gate)
    ids_ref[...] = ids.T[0:8].astype(jnp.int32)
    rank_ref[...] = rank.T[0:8].astype(jnp.int32)
    gate_ref[...] = gate
    base = base_ref[...] + jnp.sum(onehot, axis=0, keepdims=True)
    base_ref[...] = base
    cnt_ref[...] = base


def _post_call(x, ya, yt, wts, tm):
    n = x.shape[0]
    w_names = ("w_oa", "w_ob", "g_ffn", "rw_hi", "rw_lo", "rb")
    w_args = [wts[k] for k in w_names]

    def full(a):
        return pl.BlockSpec(a.shape, lambda i: (0,) * a.ndim)

    row = lambda w: pl.BlockSpec((tm, w), lambda i: (i, 0))
    choice = pl.BlockSpec((8, tm), lambda i: (0, i))
    return pl.pallas_call(
        _post_kernel, grid=(n // tm,),
        in_specs=[row(D_MODEL), row(SWA_WIDTH), pl.BlockSpec((MLA_WIDTH, tm), lambda i: (0, i))]
        + [full(a) for a in w_args],
        out_specs=(row(D_MODEL), pl.BlockSpec((tm * TILE_ROWS, LANES), lambda i: (i, 0)), choice, choice, row(LANES),
                   pl.BlockSpec((1, LANES), lambda i: (0, 0))),
        out_shape=(jax.ShapeDtypeStruct((n, D_MODEL), F32), jax.ShapeDtypeStruct((n * TILE_ROWS, LANES), F32),
                   jax.ShapeDtypeStruct((8, n), jnp.int32), jax.ShapeDtypeStruct((8, n), jnp.int32),
                   jax.ShapeDtypeStruct((n, LANES), F32), jax.ShapeDtypeStruct((1, LANES), F32)),
        scratch_shapes=[pltpu.VMEM((1, LANES), F32)],
        compiler_params=pltpu.CompilerParams(dimension_semantics=("arbitrary",), vmem_limit_bytes=VMEM_LIMIT),
        name="post",
    )(x, ya, yt, *w_args)


def _token_copy(src, src_tok, dst, dst_tok, sem):
    rows = lambda tok: pl.ds(pl.multiple_of(tok * TILE_ROWS, TILE_ROWS), TILE_ROWS)
    return pltpu.make_async_copy(src.at[rows(src_tok), :], dst.at[rows(dst_tok), :], sem)


def _dispatch_kernel(dest_ref, tail_ref, nused_ref, m_ref, xs_ref, zbuf, sem, zsem, *, td, n, te):
    block_rows = te * TILE_ROWS

    def zero_copy(slot):
        start = pl.multiple_of(slot * TILE_ROWS, block_rows)
        return pltpu.make_async_copy(zbuf, xs_ref.at[pl.ds(start, block_rows), :], zsem)

    @pl.when(pl.program_id(0) == 0)
    def _():
        zbuf[...] = jnp.zeros_like(zbuf)
        n_blocks = xs_ref.shape[0] // block_rows
        for e in range(N_EXPERTS):
            @pl.when(tail_ref[e] >= 0)
            def _():
                zero_copy(jnp.maximum(tail_ref[e], 0)).start()

        def start_unused(b, carry):
            zero_copy(b * te).start()
            return carry

        lax.fori_loop(nused_ref[0], n_blocks, start_unused, 0)
        for e in range(N_EXPERTS):
            @pl.when(tail_ref[e] >= 0)
            def _():
                zero_copy(0).wait()

        def wait_unused(b, carry):
            zero_copy(0).wait()
            return carry

        lax.fori_loop(nused_ref[0], n_blocks, wait_unused, 0)

    base = pl.program_id(0) * td

    def issue(tb, carry):
        for u in range(ROW_DMA_UNROLL):
            t = tb * ROW_DMA_UNROLL + u
            for k in range(TOP_K):
                _token_copy(m_ref, t, xs_ref, dest_ref[k * n + base + t], sem).start(priority=k % 2)
        return carry

    lax.fori_loop(0, td // ROW_DMA_UNROLL, issue, 0)

    def drain(tb, carry):
        for _u in range(ROW_DMA_UNROLL * TOP_K):
            _token_copy(m_ref, 0, xs_ref, 0, sem).wait()
        return carry

    lax.fori_loop(0, td // ROW_DMA_UNROLL, drain, 0)


def _dispatch_call(dest_flat, tail_start, n_used, m, n_slots, td, te):
    n = m.shape[0] // TILE_ROWS
    grid_spec = pltpu.PrefetchScalarGridSpec(
        num_scalar_prefetch=3, grid=(n // td,),
        in_specs=[pl.BlockSpec((td * TILE_ROWS, LANES), lambda i, d, z, u: (i, 0))],
        out_specs=pl.BlockSpec(memory_space=pl.ANY),
        scratch_shapes=[pltpu.VMEM((te * TILE_ROWS, LANES), F32), pltpu.SemaphoreType.DMA,
                        pltpu.SemaphoreType.DMA])
    return pl.pallas_call(
        functools.partial(_dispatch_kernel, td=td, n=n, te=te), grid_spec=grid_spec,
        out_shape=jax.ShapeDtypeStruct((n_slots * TILE_ROWS, LANES), F32),
        compiler_params=pltpu.CompilerParams(dimension_semantics=("arbitrary",), vmem_limit_bytes=VMEM_LIMIT),
        name="dispatch",
    )(dest_flat, tail_start, n_used, m)


def _ffn_kernel(be_ref, nused_ref, next_ref, xs_ref, bg_ref, bu_ref, bd_ref, wg_hbm, wu_hbm, wd_hbm, ys_ref,
                wbuf, wg_s, wu_s, wd_s, slot_ref, wsem, *, te):
    b = pl.program_id(0)
    used = b < nused_ref[0]
    e = be_ref[b]
    fresh = (b == 0) | (e != be_ref[jnp.maximum(b - 1, 0)])

    def weight_copies(expert, slot):
        return [pltpu.make_async_copy(w.at[expert], wbuf.at[slot, j], wsem.at[slot])
                for j, w in enumerate((wg_hbm, wu_hbm, wd_hbm))]

    @pl.when(b == 0)
    def _():
        slot_ref[0] = 0
        for c in weight_copies(e, 0):
            c.start()

    @pl.when(fresh & used)
    def _():
        slot = slot_ref[0]
        for c in weight_copies(e, slot):
            c.wait()
        nxt = next_ref[e]

        @pl.when(nxt >= 0)
        def _():
            for c in weight_copies(jnp.maximum(nxt, 0), 1 - slot):
                c.start()

        wg_s[...] = wbuf[slot, 0].astype(BF16)
        wu_s[...] = wbuf[slot, 1].astype(BF16)
        wd_s[...] = wbuf[slot, 2].astype(BF16)
        slot_ref[0] = 1 - slot

    @pl.when(used)
    def _():
        x = _load_token_tiles(xs_ref, te).astype(BF16)
        g = jnp.dot(x, wg_s[...], preferred_element_type=F32) + bg_ref[0]
        u = jnp.dot(x, wu_s[...], preferred_element_type=F32) + bu_ref[0]
        g = jnp.minimum(g, SWIGLU_LIMIT)
        u = jnp.clip(u, -SWIGLU_LIMIT, SWIGLU_LIMIT)
        act = (u + 1.0) * (g * jax.nn.sigmoid(SWIGLU_ALPHA * g))
        y = jnp.dot(act.astype(BF16), wd_s[...], preferred_element_type=F32) + bd_ref[0]
        _store_token_tiles(ys_ref, y)

    @pl.when(jnp.logical_not(used))
    def _():
        ys_ref[...] = jnp.zeros_like(ys_ref)


def _ffn_call(block_e, n_used, next_e, xs, w_gate, b_gate, w_up, b_up, w_down, b_down, te):
    n_slots = xs.shape[0] // TILE_ROWS
    bspec = lambda c: pl.BlockSpec((1, 1, c), lambda b, be, nu, nx: (be[b], 0, 0))
    hbm = pl.BlockSpec(memory_space=pl.ANY)
    grid_spec = pltpu.PrefetchScalarGridSpec(
        num_scalar_prefetch=3, grid=(n_slots // te,),
        in_specs=[pl.BlockSpec((te * TILE_ROWS, LANES), lambda b, be, nu, nx: (jnp.minimum(b, nu[0] - 1), 0)),
                  bspec(D_FF), bspec(D_FF), bspec(D_MODEL), hbm, hbm, hbm],
        out_specs=pl.BlockSpec((te * TILE_ROWS, LANES), lambda b, be, nu, nx: (b, 0)),
        scratch_shapes=[pltpu.VMEM((2, 3, D_MODEL, D_FF), F32),
                        pltpu.VMEM((D_MODEL, D_FF), BF16), pltpu.VMEM((D_MODEL, D_FF), BF16),
                        pltpu.VMEM((D_FF, D_MODEL), BF16), pltpu.SMEM((1,), jnp.int32),
                        pltpu.SemaphoreType.DMA((2,))])
    return pl.pallas_call(
        functools.partial(_ffn_kernel, te=te), grid_spec=grid_spec,
        out_shape=jax.ShapeDtypeStruct((n_slots * TILE_ROWS, LANES), F32),
        compiler_params=pltpu.CompilerParams(dimension_semantics=("arbitrary",), vmem_limit_bytes=FFN_VMEM_LIMIT),
        name="ffn",
    )(block_e, n_used, next_e, xs, b_gate[:, None, :], b_up[:, None, :], b_down[:, None, :], w_gate, w_up, w_down)


def _combine_kernel(dest_ref, h1_ref, gate_ref, ys_ref, o_ref, buf, sem, *, tc, n):
    i = pl.program_id(0)

    def gather(step, slot):
        base = step * tc

        def issue(tb, carry):
            for u in range(ROW_DMA_UNROLL):
                t = tb * ROW_DMA_UNROLL + u
                for k in range(TOP_K):
                    _token_copy(ys_ref, dest_ref[k * n + base + t], buf.at[slot, k], t,
                                sem.at[slot]).start(priority=k % 2)
            return carry

        lax.fori_loop(0, tc // ROW_DMA_UNROLL, issue, 0)

    @pl.when(i == 0)
    def _():
        gather(0, 0)

    slot = i % 2

    @pl.when(i + 1 < pl.num_programs(0))
    def _():
        gather(i + 1, 1 - slot)

    def drain(tb, carry):
        for _u in range(ROW_DMA_UNROLL):
            for k in range(TOP_K):
                _token_copy(ys_ref, 0, buf.at[slot, k], 0, sem.at[slot]).wait()
        return carry

    lax.fori_loop(0, tc // ROW_DMA_UNROLL, drain, 0)
    gate = gate_ref[...]
    out = h1_ref[...]
    for k in range(TOP_K):
        out = out + gate[:, k:k + 1] * _load_token_tiles(buf.at[slot, k], tc)
    o_ref[...] = out


def _combine_call(dest_flat, h1, gate, ys, tc):
    n = h1.shape[0]
    grid_spec = pltpu.PrefetchScalarGridSpec(
        num_scalar_prefetch=1, grid=(n // tc,),
        in_specs=[pl.BlockSpec((tc, D_MODEL), lambda i, d: (i, 0)),
                  pl.BlockSpec((tc, LANES), lambda i, d: (i, 0)),
                  pl.BlockSpec(memory_space=pl.ANY)],
        out_specs=pl.BlockSpec((tc, D_MODEL), lambda i, d: (i, 0)),
        scratch_shapes=[pltpu.VMEM((2, TOP_K, tc * TILE_ROWS, LANES), F32), pltpu.SemaphoreType.DMA((2,))])
    return pl.pallas_call(
        functools.partial(_combine_kernel, tc=tc, n=n), grid_spec=grid_spec,
        out_shape=jax.ShapeDtypeStruct((n, D_MODEL), F32),
        compiler_params=pltpu.CompilerParams(dimension_semantics=("arbitrary",), vmem_limit_bytes=VMEM_LIMIT),
        name="combine",
    )(dest_flat, h1, gate, ys)


def _prepare_weights(attn_norm_g, w_in, swa_q_norm_g, swa_k_norm_g, mla_q_lat_norm_g, w_mla_q_up,
                     mla_kv_lat_norm_g, w_mla_kv_up, mla_q_norm_g, mla_k_norm_g, w_out, ffn_norm_g,
                     router_w, router_b):
    o_k = SWA_WIDTH
    o_v = o_k + SWA_KV_WIDTH
    o_ql = o_v + SWA_KV_WIDTH
    o_kv = o_ql + MLA_Q_LORA
    o_kr = o_kv + MLA_KV_LORA
    perm = np.arange(SWA_HEADS).reshape(SWA_KV_HEADS, SWA_GROUP).T.reshape(-1)
    w_qs = w_in[:, :o_k].reshape(D_MODEL, SWA_HEADS, SWA_HEAD_DIM)[:, perm].reshape(D_MODEL, SWA_WIDTH)
    w_kr = jnp.zeros((D_MODEL, LANES), F32).at[:, MLA_NOPE_DIM:MLA_QK_DIM].set(w_in[:, o_kr:])
    half = MLA_ROPE_DIM // 2
    r0, r1, r2 = MLA_NOPE_DIM, MLA_NOPE_DIM + half, MLA_QK_DIM

    def partner_lanes(a):
        z = jnp.zeros_like(a)
        return z.at[..., r0:r1].set(a[..., r1:r2]).at[..., r1:r2].set(a[..., r0:r1])

    w_in_r = jnp.concatenate([w_qs, w_in[:, o_k:o_kr], w_kr, partner_lanes(w_kr)], axis=1).astype(BF16)

    pad_qk = LANES - MLA_QK_DIM
    w_qup = jnp.pad(w_mla_q_up.reshape(MLA_Q_LORA, MLA_HEADS, MLA_QK_DIM), ((0, 0), (0, 0), (0, pad_qk)))
    w_qup = jnp.concatenate([w_qup.reshape(MLA_Q_LORA, MLA_HEADS * LANES),
                             partner_lanes(w_qup).reshape(MLA_Q_LORA, MLA_HEADS * LANES)], axis=1).astype(BF16)
    kv_up = w_mla_kv_up.reshape(MLA_KV_LORA, MLA_HEADS, MLA_NOPE_DIM + MLA_V_DIM)
    w_k = jnp.pad(kv_up[:, :, :MLA_NOPE_DIM], ((0, 0), (0, 0), (0, LANES - MLA_NOPE_DIM)))
    w_k = w_k.reshape(MLA_KV_LORA, MLA_HEADS * LANES).astype(BF16)
    w_vt = jnp.transpose(kv_up[:, :, MLA_NOPE_DIM:], (1, 2, 0))
    w_vt = jnp.pad(w_vt, ((0, 0), (0, VT_ROWS - MLA_V_DIM), (0, 0))).reshape(MLA_HEADS * VT_ROWS, MLA_KV_LORA)
    ones = np.zeros((MLA_HEADS, VT_ROWS, 1), np.float32)
    ones[:, MLA_V_DIM, 0] = 1.0

    w_oa = w_out[:SWA_WIDTH].reshape(SWA_HEADS, SWA_HEAD_DIM, D_MODEL)[perm].reshape(SWA_WIDTH, D_MODEL)
    rw = jnp.pad(router_w, ((0, 0), (0, LANES - N_EXPERTS)))
    rw_hi = rw.astype(BF16)
    rw_lo = (rw - rw_hi.astype(F32)).astype(BF16)
    rb = jnp.concatenate([router_b.astype(F32), jnp.full((LANES - N_EXPERTS,), NEG_BIG, F32)])[None, :]
    return {
        "g_attn": attn_norm_g[None, :], "w_in": w_in_r,
        "g_qs": jnp.tile(swa_q_norm_g, 2)[None, :] * (SWA_HEAD_DIM ** -0.5),
        "g_ks": jnp.tile(swa_k_norm_g, 2)[None, :],
        "g_qlat": mla_q_lat_norm_g[None, :], "w_qup": w_qup,
        "g_kvlat": mla_kv_lat_norm_g[None, :], "w_k": w_k, "w_vt": w_vt.astype(BF16),
        "ones": jnp.asarray(ones.reshape(MLA_HEADS * VT_ROWS, 1)),
        "g_qm": jnp.pad(mla_q_norm_g, (0, pad_qk))[None, :] * (MLA_QK_DIM ** -0.5 * LOG2_E),
        "g_km": jnp.pad(mla_k_norm_g, (0, pad_qk))[None, :],
        "g_qm_rot": partner_lanes(jnp.pad(mla_q_norm_g, (0, pad_qk))[None, :] * (MLA_QK_DIM ** -0.5 * LOG2_E)),
        "g_km_rot": partner_lanes(jnp.pad(mla_k_norm_g, (0, pad_qk))[None, :]),
        "w_oa": w_oa.astype(BF16), "w_ob": w_out[SWA_WIDTH:].astype(BF16),
        "g_ffn": ffn_norm_g[None, :], "rw_hi": rw_hi, "rw_lo": rw_lo, "rb": rb,
    }


def _rope_tables(n_rows):
    half = MLA_ROPE_DIM // 2
    inv = (1.0 / (ROPE_THETA ** (np.arange(half, dtype=np.float32) / half))).astype(np.float32)
    ang = np.arange(n_rows, dtype=np.float32)[:, None] * inv[None, :]
    cos, sin = jnp.asarray(np.cos(ang)), jnp.asarray(np.sin(ang))
    one = jnp.ones((n_rows, MLA_NOPE_DIM), F32)
    zero = jnp.zeros((n_rows, MLA_NOPE_DIM), F32)
    pad1 = jnp.ones((n_rows, LANES - MLA_QK_DIM), F32)
    pad0 = jnp.zeros((n_rows, LANES - MLA_QK_DIM), F32)
    return (jnp.concatenate([one, cos, cos, pad1], axis=1),
            jnp.concatenate([zero, -sin, sin, pad0], axis=1))


def kernel(x, meta_tokens, attn_norm_g, w_in, swa_q_norm_g, swa_k_norm_g, swa_sink, mla_q_lat_norm_g, w_mla_q_up, mla_kv_lat_norm_g, w_mla_kv_up, mla_q_norm_g, mla_k_norm_g, w_out, ffn_norm_g, router_w, router_b, w_gate, b_gate, w_up, b_up, w_down, b_down):
    bsz, n, d = x.shape
    assert bsz == 1 and d == D_MODEL and n % TK_MLA == 0 and attn_norm_g.shape[0] == 1
    wts = _prepare_weights(attn_norm_g[0], w_in[0], swa_q_norm_g[0], swa_k_norm_g[0], mla_q_lat_norm_g[0],
                           w_mla_q_up[0], mla_kv_lat_norm_g[0], w_mla_kv_up[0], mla_q_norm_g[0],
                           mla_k_norm_g[0], w_out[0], ffn_norm_g[0], router_w[0], router_b[0])
    xr = x[0]
    cos_t, sin_t = _rope_tables(N_META + n)

    xm = jnp.pad(meta_tokens.astype(F32), ((0, LANES - N_META), (0, 0)))
    valid_m = (jnp.arange(LANES) < N_META).astype(F32)[None, :]
    _, ks_m, vs_m, _, km_m, vt_m = _proj_call(xm, valid_m, cos_t[:LANES], sin_t[:LANES], wts, LANES)
    qs, ks, vs, qm, km, vt = _proj_call(xr, jnp.ones((1, n), F32), cos_t[N_META:], sin_t[N_META:], wts, TM_PROJ)

    ya = _swa_call(swa_sink[0], qs, ks, vs, ks_m, vs_m)
    yt = _mla_call(qm, km, vt, km_m, vt_m, TQ_MLA, TK_MLA)
    h1, m, ids, rank, gate, cnt = _post_call(xr, ya, yt, wts, TM_POST)

    te = TE_FFN
    counts = cnt[0, :N_EXPERTS].astype(jnp.int32)
    padded = (counts + te - 1) // te * te
    pad_end = jnp.cumsum(padded)
    pad_start = pad_end - padded
    expert = jnp.arange(N_EXPERTS, dtype=jnp.int32)[:, None, None]
    start_of = jnp.sum(jnp.where(ids[None, :TOP_K] == expert, pad_start[:, None, None], 0), axis=0)
    dest = (start_of + rank[:TOP_K]).reshape(-1)
    n_blocks = (n * TOP_K + N_EXPERTS * (te - 1)) // te
    block_start = jnp.arange(n_blocks, dtype=jnp.int32) * te
    block_e = jnp.minimum(jnp.sum((pad_end[None, :] <= block_start[:, None]).astype(jnp.int32), axis=1),
                          N_EXPERTS - 1)
    n_used = (pad_end[-1:] // te).astype(jnp.int32)
    tail_start = jnp.where(padded > 0, pad_end - te, -1).astype(jnp.int32)
    eid = jnp.arange(N_EXPERTS, dtype=jnp.int32)
    later_used = (padded[None, :] > 0) & (eid[None, :] > eid[:, None])
    next_e = jnp.min(jnp.where(later_used, eid[None, :], N_EXPERTS), axis=1)
    next_e = jnp.where(next_e < N_EXPERTS, next_e, -1).astype(jnp.int32)

    xs = _dispatch_call(dest, tail_start, n_used, m, n_blocks * te, TD_DISPATCH, te)
    ys = _ffn_call(block_e, n_used, next_e, xs, w_gate[0], b_gate[0], w_up[0], b_up[0], w_down[0], b_down[0], te)
    out = _combine_call(dest, h1, gate, ys, TC_COMBINE)
    return out[None]
```

```python
import functools

import numpy as np
import jax
import jax.numpy as jnp
from jax import lax
from jax.experimental import pallas as pl
from jax.experimental.pallas import tpu as pltpu

F32 = jnp.float32
BF16 = jnp.bfloat16

D_MODEL = 1024
N_META = 16
BLOCK = 128
WINDOW = 128
NORM_EPS = 1e-6

SWA_HEADS = 8
SWA_KV_HEADS = 2
SWA_GROUP = SWA_HEADS // SWA_KV_HEADS
SWA_HEAD_DIM = 64
SWA_WIDTH = SWA_HEADS * SWA_HEAD_DIM
SWA_KV_WIDTH = SWA_KV_HEADS * SWA_HEAD_DIM

MLA_HEADS = 8
MLA_Q_LORA = 256
MLA_KV_LORA = 128
MLA_NOPE_DIM = 64
MLA_ROPE_DIM = 32
MLA_V_DIM = 64
MLA_QK_DIM = MLA_NOPE_DIM + MLA_ROPE_DIM
MLA_WIDTH = MLA_HEADS * MLA_V_DIM
ROPE_THETA = 10000.0

N_EXPERTS = 32
TOP_K = 4
D_FF = 1024
SWIGLU_LIMIT = 7.0
SWIGLU_ALPHA = 1.702

LANES = 128
VT_ROWS = 80
NEG_BIG = -1e30
LOG2_E = 1.4426950408889634
VMEM_LIMIT = 48 * 1024 * 1024
FFN_VMEM_LIMIT = 56 * 1024 * 1024

TM_PROJ = 512
TQ_MLA = 512
TK_MLA = 512
MLA_UNROLL = 16
ROW_DMA_UNROLL = 8
TM_POST = 512
TD_DISPATCH = 128
TE_FFN = 256
TC_COMBINE = 128

_NT = (((1,), (1,)), ((), ()))


def _rms(v, n):
    return v * lax.rsqrt(jnp.sum(v * v, axis=-1, keepdims=True) * (1.0 / n) + NORM_EPS)


TILE_ROWS = D_MODEL // LANES


def _store_token_tiles(ref, v):
    rows = v.shape[0]
    for s in range(TILE_ROWS):
        ref[pl.ds(s, rows, stride=TILE_ROWS), :] = v[:, s * LANES:(s + 1) * LANES]


def _load_token_tiles(ref, rows):
    return jnp.concatenate([ref[pl.ds(s, rows, stride=TILE_ROWS), :] for s in range(TILE_ROWS)], axis=1)


def _proj_kernel(x_ref, valid_ref, cos_ref, sin_ref, g_attn_ref, w_in_ref, g_qs_ref, g_ks_ref,
                 g_qlat_ref, w_qup_ref, g_kvlat_ref, w_k_ref, w_vt_ref, ones_ref, g_qm_ref, g_km_ref,
                 g_qm_rot_ref, g_km_rot_ref,
                 qs_ref, ks_ref, vs_ref, qm_ref, km_ref, vt_ref):
    tm = x_ref.shape[0]
    a = _rms(x_ref[...], D_MODEL) * g_attn_ref[...]
    proj = jnp.dot(a.astype(BF16), w_in_ref[...], preferred_element_type=F32)

    lane = lax.broadcasted_iota(jnp.int32, (tm, LANES), 1)
    lo = lane < SWA_HEAD_DIM

    def seg_norm(v, g):
        v2 = v * v
        s_all = jnp.sum(v2, axis=-1, keepdims=True)
        s_lo = jnp.sum(jnp.where(lo, v2, 0.0), axis=-1, keepdims=True)
        ms = jnp.where(lo, s_lo, s_all - s_lo) * (1.0 / SWA_HEAD_DIM)
        return v * lax.rsqrt(ms + NORM_EPS) * g

    g_qs = g_qs_ref[...]
    for j in range(SWA_WIDTH // LANES):
        sl = slice(j * LANES, (j + 1) * LANES)
        qs_ref[:, sl] = seg_norm(proj[:, sl], g_qs).astype(BF16)
    o_k = SWA_WIDTH
    o_v = o_k + SWA_KV_WIDTH
    o_ql = o_v + SWA_KV_WIDTH
    o_kv = o_ql + MLA_Q_LORA
    o_kr = o_kv + MLA_KV_LORA
    ks_ref[...] = seg_norm(proj[:, o_k:o_v], g_ks_ref[...]).astype(BF16)
    vs_ref[...] = proj[:, o_v:o_ql].astype(BF16)

    cosv = cos_ref[...]
    sinv = sin_ref[...]

    def norm_rope(v, v_rot, g, g_rot):
        ms = jnp.sum(v * v, axis=-1, keepdims=True) * (1.0 / MLA_QK_DIM)
        return (v * g * cosv + v_rot * g_rot * sinv) * lax.rsqrt(ms + NORM_EPS)

    hw = MLA_HEADS * LANES
    qln = _rms(proj[:, o_ql:o_kv], MLA_Q_LORA) * g_qlat_ref[...]
    qup = jnp.dot(qln.astype(BF16), w_qup_ref[...], preferred_element_type=F32)
    g_qm = g_qm_ref[...]
    g_qm_rot = g_qm_rot_ref[...]
    for h in range(MLA_HEADS):
        sl = slice(h * LANES, (h + 1) * LANES)
        sl_rot = slice(hw + h * LANES, hw + (h + 1) * LANES)
        qm_ref[h] = norm_rope(qup[:, sl], qup[:, sl_rot], g_qm, g_qm_rot).T.astype(BF16)

    kvn = (_rms(proj[:, o_kv:o_kr], MLA_KV_LORA) * g_kvlat_ref[...]).astype(BF16)
    knope = jnp.dot(kvn, w_k_ref[...], preferred_element_type=F32)
    krope = proj[:, o_kr:o_kr + LANES]
    krope_rot = proj[:, o_kr + LANES:o_kr + 2 * LANES]
    g_km = g_km_ref[...]
    g_km_rot = g_km_rot_ref[...]
    for h in range(MLA_HEADS):
        km_ref[h] = norm_rope(knope[:, h * LANES:(h + 1) * LANES] + krope, krope_rot, g_km, g_km_rot).astype(BF16)

    vt = lax.dot_general(w_vt_ref[...], kvn, _NT, preferred_element_type=F32)
    vt = vt + ones_ref[...] * valid_ref[...]
    for h in range(MLA_HEADS):
        vt_ref[h] = vt[h * VT_ROWS:(h + 1) * VT_ROWS].astype(BF16)


def _proj_call(x, valid, cos_t, sin_t, wts, tm):
    n = x.shape[0]
    w_names = ("g_attn", "w_in", "g_qs", "g_ks", "g_qlat", "w_qup", "g_kvlat", "w_k", "w_vt", "ones", "g_qm", "g_km",
               "g_qm_rot", "g_km_rot")
    w_args = [wts[k] for k in w_names]

    def full(a):
        return pl.BlockSpec(a.shape, lambda i: (0,) * a.ndim)

    in_specs = [pl.BlockSpec((tm, D_MODEL), lambda i: (i, 0)),
                pl.BlockSpec((1, tm), lambda i: (0, i)),
                pl.BlockSpec((tm, LANES), lambda i: (i, 0)),
                pl.BlockSpec((tm, LANES), lambda i: (i, 0))] + [full(a) for a in w_args]
    out_shape = (jax.ShapeDtypeStruct((n, SWA_WIDTH), BF16),
                 jax.ShapeDtypeStruct((n, SWA_KV_WIDTH), BF16),
                 jax.ShapeDtypeStruct((n, SWA_KV_WIDTH), BF16),
                 jax.ShapeDtypeStruct((MLA_HEADS, LANES, n), BF16),
                 jax.ShapeDtypeStruct((MLA_HEADS, n, LANES), BF16),
                 jax.ShapeDtypeStruct((MLA_HEADS, VT_ROWS, n), BF16))
    out_specs = (pl.BlockSpec((tm, SWA_WIDTH), lambda i: (i, 0)),
                 pl.BlockSpec((tm, SWA_KV_WIDTH), lambda i: (i, 0)),
                 pl.BlockSpec((tm, SWA_KV_WIDTH), lambda i: (i, 0)),
                 pl.BlockSpec((MLA_HEADS, LANES, tm), lambda i: (0, 0, i)),
                 pl.BlockSpec((MLA_HEADS, tm, LANES), lambda i: (0, i, 0)),
                 pl.BlockSpec((MLA_HEADS, VT_ROWS, tm), lambda i: (0, 0, i)))
    return pl.pallas_call(
        _proj_kernel, grid=(n // tm,), in_specs=in_specs, out_specs=out_specs, out_shape=out_shape,
        compiler_params=pltpu.CompilerParams(dimension_semantics=("arbitrary",), vmem_limit_bytes=VMEM_LIMIT),
        name="proj",
    )(x, valid, cos_t, sin_t, *w_args)


def _swa_bias_table():
    nkeys = 3 * BLOCK + LANES
    r = np.arange(BLOCK)[:, None]
    c = np.arange(nkeys)[None, :]
    d = np.abs(BLOCK + r - c)
    in_win = (d <= WINDOW) & (c < 3 * BLOCK)
    is_meta = (c >= 3 * BLOCK) & (c < 3 * BLOCK + N_META)
    slopes = 2.0 ** (-8.0 * np.arange(1, SWA_HEADS + 1) / SWA_HEADS)
    bias = np.where(in_win[None], -slopes[:, None, None] * d[None], np.where(is_meta[None], 0.0, NEG_BIG))
    return jnp.asarray(bias, F32)


def _swa_kernel(sink_ref, q_ref, kp_ref, kc_ref, kn_ref, vp_ref, vc_ref, vn_ref, km_ref, vm_ref, bias_ref, o_ref, s_scr):
    b = pl.program_id(0)
    nb = pl.num_programs(0)
    nkeys = 3 * BLOCK + LANES
    kall = jnp.concatenate([kp_ref[...], kc_ref[...], kn_ref[...], km_ref[...]], axis=0)
    vall = jnp.concatenate([vp_ref[...], vc_ref[...], vn_ref[...], vm_ref[...]], axis=0)

    c = lax.broadcasted_iota(jnp.int32, (1, nkeys), 1)
    lo_edge = jnp.where(b > 0, 0, BLOCK)
    hi_edge = jnp.where(b < nb - 1, 3 * BLOCK, 2 * BLOCK)
    edge = jnp.where((c < lo_edge) | ((c >= hi_edge) & (c < 3 * BLOCK)), NEG_BIG, 0.0)

    lane = lax.broadcasted_iota(jnp.int32, (BLOCK, LANES), 1)
    lo = lane < SWA_HEAD_DIM
    for j in range(SWA_GROUP):
        qg = q_ref[:, j * LANES:(j + 1) * LANES].astype(F32)
        q2 = jnp.concatenate([jnp.where(lo, qg, 0.0), jnp.where(lo, 0.0, qg)], axis=0).astype(BF16)
        s_scr[j] = lax.dot_general(q2, kall, _NT, preferred_element_type=F32)
    for j in range(SWA_GROUP):
        outs = []
        for half, h in ((0, j), (1, j + SWA_GROUP)):
            sh = s_scr[j, half * BLOCK:(half + 1) * BLOCK] + bias_ref[h] + edge
            sink = sink_ref[h]
            m = jnp.maximum(jnp.max(sh, axis=-1, keepdims=True), sink)
            p = jnp.exp(sh - m)
            den = jnp.sum(p, axis=-1, keepdims=True) + jnp.exp(sink - m)
            o = jnp.dot(p.astype(BF16), vall, preferred_element_type=F32)
            outs.append(o / den)
        o_ref[:, j * LANES:(j + 1) * LANES] = jnp.where(lo, outs[0], outs[1]).astype(BF16)


def _swa_call(sink, qs, ks, vs, ks_meta, vs_meta):
    n = qs.shape[0]
    nb = n // BLOCK
    bias = _swa_bias_table()
    kv_prev = pl.BlockSpec((BLOCK, SWA_KV_WIDTH), lambda b: (jnp.maximum(b - 1, 0), 0))
    kv_cur = pl.BlockSpec((BLOCK, SWA_KV_WIDTH), lambda b: (b, 0))
    kv_next = pl.BlockSpec((BLOCK, SWA_KV_WIDTH), lambda b: (jnp.minimum(b + 1, nb - 1), 0))
    meta = pl.BlockSpec((LANES, SWA_KV_WIDTH), lambda b: (0, 0))
    return pl.pallas_call(
        _swa_kernel,
        grid=(nb,),
        in_specs=[pl.BlockSpec(memory_space=pltpu.SMEM),
                  pl.BlockSpec((BLOCK, SWA_WIDTH), lambda b: (b, 0)),
                  kv_prev, kv_cur, kv_next, kv_prev, kv_cur, kv_next, meta, meta,
                  pl.BlockSpec(bias.shape, lambda b: (0, 0, 0))],
        out_specs=pl.BlockSpec((BLOCK, SWA_WIDTH), lambda b: (b, 0)),
        out_shape=jax.ShapeDtypeStruct((n, SWA_WIDTH), BF16),
        scratch_shapes=[pltpu.VMEM((SWA_GROUP, 2 * BLOCK, 3 * BLOCK + LANES), F32)],
        compiler_params=pltpu.CompilerParams(dimension_semantics=("arbitrary",), vmem_limit_bytes=VMEM_LIMIT),
        name="swa",
    )(sink, qs, ks, ks, ks, vs, vs, vs, ks_meta, vs_meta, bias)


def _mla_kernel(q_ref, k_ref, vt_ref, km_ref, vtm_ref, o_ref, s_scr, *, tk, nk, unroll):
    qt = q_ref[0]
    tq = qt.shape[1]

    def scores(off):
        return jnp.dot(k_ref[0, pl.ds(off, tk), :], qt, preferred_element_type=F32)

    def soft_pv(s, vt, m, acc):
        m_new = jnp.maximum(m, jnp.max(s, axis=0, keepdims=True))
        alpha = jnp.exp2(m - m_new)
        p = jnp.exp2(s - m_new).astype(BF16)
        acc = alpha * acc + jnp.dot(vt, p, preferred_element_type=F32)
        return m_new, acc

    m0 = jnp.full((1, tq), NEG_BIG, F32)
    acc0 = jnp.zeros((VT_ROWS, tq), F32)
    s_meta = jnp.dot(km_ref[0], qt, preferred_element_type=F32)
    s_scr[0] = scores(0)
    m, acc = soft_pv(s_meta, vtm_ref[0], m0, acc0)

    def body(i, carry):
        m, acc = carry
        for u in range(unroll):
            t = i * unroll + u
            cur = u % 2
            s_next = scores(pl.multiple_of(jnp.minimum(t + 1, nk - 1) * tk, tk))
            off = pl.multiple_of(t * tk, tk)
            m, acc = soft_pv(s_scr[cur], vt_ref[0, :, pl.ds(off, tk)], m, acc)
            s_scr[1 - cur] = s_next
        return m, acc

    m, acc = lax.fori_loop(0, nk // unroll, body, (m, acc))
    o_ref[...] = (acc[0:MLA_V_DIM] / acc[MLA_V_DIM:MLA_V_DIM + 1]).astype(BF16)


def _mla_call(qm, km, vt, km_meta, vt_meta, tq, tk):
    n = qm.shape[2]
    nk = n // tk
    unroll = MLA_UNROLL if nk % MLA_UNROLL == 0 else 2
    assert nk % unroll == 0
    return pl.pallas_call(
        functools.partial(_mla_kernel, tk=tk, nk=nk, unroll=unroll),
        grid=(MLA_HEADS, n // tq),
        in_specs=[pl.BlockSpec((1, LANES, tq), lambda h, i: (h, 0, i)),
                  pl.BlockSpec((1, n, LANES), lambda h, i: (h, 0, 0)),
                  pl.BlockSpec((1, VT_ROWS, n), lambda h, i: (h, 0, 0)),
                  pl.BlockSpec((1, LANES, LANES), lambda h, i: (h, 0, 0)),
                  pl.BlockSpec((1, VT_ROWS, LANES), lambda h, i: (h, 0, 0))],
        out_specs=pl.BlockSpec((MLA_V_DIM, tq), lambda h, i: (h, i)),
        out_shape=jax.ShapeDtypeStruct((MLA_WIDTH, n), BF16),
        scratch_shapes=[pltpu.VMEM((2, tk, tq), F32)],
        compiler_params=pltpu.CompilerParams(dimension_semantics=("arbitrary", "arbitrary"),
                                             vmem_limit_bytes=VMEM_LIMIT),
        name="mla",
    )(qm, km, vt, km_meta, vt_meta)


def _post_kernel(x_ref, ya_ref, yt_ref, w_oa_ref, w_ob_ref, g_ffn_ref, rw_hi_ref, rw_lo_ref, rb_ref,
                 h1_ref, m_ref, ids_ref, rank_ref, gate_ref, cnt_ref, base_ref):
    tm = x_ref.shape[0]

    @pl.when(pl.program_id(0) == 0)
    def _():
        base_ref[...] = jnp.zeros_like(base_ref)

    yb = yt_ref[...].T
    h1 = (x_ref[...]
          + jnp.dot(ya_ref[...], w_oa_ref[...], preferred_element_type=F32)
          + jnp.dot(yb, w_ob_ref[...], preferred_element_type=F32))
    h1_ref[...] = h1
    m = _rms(h1, D_MODEL) * g_ffn_ref[...]
    _store_token_tiles(m_ref, m)

    m_hi = m.astype(BF16)
    m_lo = (m - m_hi.astype(F32)).astype(BF16)
    rw_hi = rw_hi_ref[...]
    logits = (jnp.dot(m_hi, rw_hi, preferred_element_type=F32)
              + jnp.dot(m_lo, rw_hi, preferred_element_type=F32)
              + jnp.dot(m_hi, rw_lo_ref[...], preferred_element_type=F32)
              + rb_ref[...])

    lane = lax.broadcasted_iota(jnp.int32, (tm, LANES), 1)
    lane_f = lane.astype(F32)
    work = logits
    vals, idxs = [], []
    for _k in range(TOP_K):
        mx = jnp.max(work, axis=-1, keepdims=True)
        ix = jnp.min(jnp.where(work == mx, lane_f, float(LANES)), axis=-1, keepdims=True)
        vals.append(mx)
        idxs.append(ix)
        work = jnp.where(lane_f == ix, -3e38, work)
    exps = [jnp.exp(v - vals[0]) for v in vals]
    den = exps[0] + exps[1] + exps[2] + exps[3]

    hits = [lane_f == ix for ix in idxs]
    onehot = jnp.zeros((tm, LANES), F32)
    for hk in hits:
        onehot = onehot + jnp.where(hk, 1.0, 0.0)
    row = lax.broadcasted_iota(jnp.int32, (tm, tm), 0)
    col = lax.broadcasted_iota(jnp.int32, (tm, tm), 1)
    tri = jnp.where(row > col, 1.0, 0.0).astype(BF16)
    before = jnp.dot(tri, onehot.astype(BF16), preferred_element_type=F32) + base_ref[...]
    ids = jnp.zeros((tm, LANES), F32)
    rank = jnp.zeros((tm, LANES), F32)
    gate = jnp.zeros((tm, LANES), F32)
    for k in range(TOP_K):
        rk = jnp.sum(jnp.where(hits[k], before, 0.0), axis=-1, keepdims=True)
        sel = lane == k
        ids = jnp.where(sel, idxs[k], ids)
        rank = jnp.where(sel, rk, rank)
        gate = jnp.where(sel, exps[k] / den, gate)
    ids_ref[...] = ids.T[0:8].astype(jnp.int32)
    rank_ref[...] = rank.T[0:8].astype(jnp.int32)
    gate_ref[...] = gate
    base = base_ref[...] + jnp.sum(onehot, axis=0, keepdims=True)
    base_ref[...] = base
    cnt_ref[...] = base


def _post_call(x, ya, yt, wts, tm):
    n = x.shape[0]
    w_names = ("w_oa", "w_ob", "g_ffn", "rw_hi", "rw_lo", "rb")
    w_args = [wts[k] for k in w_names]

    def full(a):
        return pl.BlockSpec(a.shape, lambda i: (0,) * a.ndim)

    row = lambda w: pl.BlockSpec((tm, w), lambda i: (i, 0))
    choice = pl.BlockSpec((8, tm), lambda i: (0, i))
    return pl.pallas_call(
        _post_kernel, grid=(n // tm,),
        in_specs=[row(D_MODEL), row(SWA_WIDTH), pl.BlockSpec((MLA_WIDTH, tm), lambda i: (0, i))]
        + [full(a) for a in w_args],
        out_specs=(row(D_MODEL), pl.BlockSpec((tm * TILE_ROWS, LANES), lambda i: (i, 0)), choice, choice, row(LANES),
                   pl.BlockSpec((1, LANES), lambda i: (0, 0))),
        out_shape=(jax.ShapeDtypeStruct((n, D_MODEL), F32), jax.ShapeDtypeStruct((n * TILE_ROWS, LANES), F32),
                   jax.ShapeDtypeStruct((8, n), jnp.int32), jax.ShapeDtypeStruct((8, n), jnp.int32),
                   jax.ShapeDtypeStruct((n, LANES), F32), jax.ShapeDtypeStruct((1, LANES), F32)),
        scratch_shapes=[pltpu.VMEM((1, LANES), F32)],
        compiler_params=pltpu.CompilerParams(dimension_semantics=("arbitrary",), vmem_limit_bytes=VMEM_LIMIT),
        name="post",
    )(x, ya, yt, *w_args)


def _token_copy(src, src_tok, dst, dst_tok, sem):
    rows = lambda tok: pl.ds(pl.multiple_of(tok * TILE_ROWS, TILE_ROWS), TILE_ROWS)
    return pltpu.make_async_copy(src.at[rows(src_tok), :], dst.at[rows(dst_tok), :], sem)


def _dispatch_kernel(dest_ref, tail_ref, nused_ref, m_ref, xs_ref, zbuf, sem, zsem, *, td, n, te):
    block_rows = te * TILE_ROWS

    def zero_copy(slot):
        start = pl.multiple_of(slot * TILE_ROWS, block_rows)
        return pltpu.make_async_copy(zbuf, xs_ref.at[pl.ds(start, block_rows), :], zsem)

    @pl.when(pl.program_id(0) == 0)
    def _():
        zbuf[...] = jnp.zeros_like(zbuf)
        n_blocks = xs_ref.shape[0] // block_rows
        for e in range(N_EXPERTS):
            @pl.when(tail_ref[e] >= 0)
            def _():
                zero_copy(jnp.maximum(tail_ref[e], 0)).start()

        def start_unused(b, carry):
            zero_copy(b * te).start()
            return carry

        lax.fori_loop(nused_ref[0], n_blocks, start_unused, 0)
        for e in range(N_EXPERTS):
            @pl.when(tail_ref[e] >= 0)
            def _():
                zero_copy(0).wait()

        def wait_unused(b, carry):
            zero_copy(0).wait()
            return carry

        lax.fori_loop(nused_ref[0], n_blocks, wait_unused, 0)

    base = pl.program_id(0) * td

    def issue(tb, carry):
        for u in range(ROW_DMA_UNROLL):
            t = tb * ROW_DMA_UNROLL + u
            for k in range(TOP_K):
                _token_copy(m_ref, t, xs_ref, dest_ref[k * n + base + t], sem).start(priority=k % 2)
        return carry

    lax.fori_loop(0, td // ROW_DMA_UNROLL, issue, 0)

    def drain(tb, carry):
        for _u in range(ROW_DMA_UNROLL * TOP_K):
            _token_copy(m_ref, 0, xs_ref, 0, sem).wait()
        return carry

    lax.fori_loop(0, td // ROW_DMA_UNROLL, drain, 0)


def _dispatch_call(dest_flat, tail_start, n_used, m, n_slots, td, te):
    n = m.shape[0] // TILE_ROWS
    grid_spec = pltpu.PrefetchScalarGridSpec(
        num_scalar_prefetch=3, grid=(n // td,),
        in_specs=[pl.BlockSpec((td * TILE_ROWS, LANES), lambda i, d, z, u: (i, 0))],
        out_specs=pl.BlockSpec(memory_space=pl.ANY),
        scratch_shapes=[pltpu.VMEM((te * TILE_ROWS, LANES), F32), pltpu.SemaphoreType.DMA,
                        pltpu.SemaphoreType.DMA])
    return pl.pallas_call(
        functools.partial(_dispatch_kernel, td=td, n=n, te=te), grid_spec=grid_spec,
        out_shape=jax.ShapeDtypeStruct((n_slots * TILE_ROWS, LANES), F32),
        compiler_params=pltpu.CompilerParams(dimension_semantics=("arbitrary",), vmem_limit_bytes=VMEM_LIMIT),
        name="dispatch",
    )(dest_flat, tail_start, n_used, m)


def _ffn_kernel(be_ref, nused_ref, next_ref, xs_ref, bg_ref, bu_ref, bd_ref, wg_hbm, wu_hbm, wd_hbm, ys_ref,
                wbuf, wg_s, wu_s, wd_s, slot_ref, wsem, *, te):
    b = pl.program_id(0)
    used = b < nused_ref[0]
    e = be_ref[b]
    fresh = (b == 0) | (e != be_ref[jnp.maximum(b - 1, 0)])

    def weight_copies(expert, slot):
        return [pltpu.make_async_copy(w.at[expert], wbuf.at[slot, j], wsem.at[slot])
                for j, w in enumerate((wg_hbm, wu_hbm, wd_hbm))]

    @pl.when(b == 0)
    def _():
        slot_ref[0] = 0
        for c in weight_copies(e, 0):
            c.start()

    @pl.when(fresh & used)
    def _():
        slot = slot_ref[0]
        for c in weight_copies(e, slot):
            c.wait()
        nxt = next_ref[e]

        @pl.when(nxt >= 0)
        def _():
            for c in weight_copies(jnp.maximum(nxt, 0), 1 - slot):
                c.start()

        wg_s[...] = wbuf[slot, 0].astype(BF16)
        wu_s[...] = wbuf[slot, 1].astype(BF16)
        wd_s[...] = wbuf[slot, 2].astype(BF16)
        slot_ref[0] = 1 - slot

    @pl.when(used)
    def _():
        x = _load_token_tiles(xs_ref, te).astype(BF16)
        g = jnp.dot(x, wg_s[...], preferred_element_type=F32) + bg_ref[0]
        u = jnp.dot(x, wu_s[...], preferred_element_type=F32) + bu_ref[0]
        g = jnp.minimum(g, SWIGLU_LIMIT)
        u = jnp.clip(u, -SWIGLU_LIMIT, SWIGLU_LIMIT)
        act = (u + 1.0) * (g * jax.nn.sigmoid(SWIGLU_ALPHA * g))
        y = jnp.dot(act.astype(BF16), wd_s[...], preferred_element_type=F32) + bd_ref[0]
        _store_token_tiles(ys_ref, y)

    @pl.when(jnp.logical_not(used))
    def _():
        ys_ref[...] = jnp.zeros_like(ys_ref)


def _ffn_call(block_e, n_used, next_e, xs, w_gate, b_gate, w_up, b_up, w_down, b_down, te):
    n_slots = xs.shape[0] // TILE_ROWS
    bspec = lambda c: pl.BlockSpec((1, 1, c), lambda b, be, nu, nx: (be[b], 0, 0))
    hbm = pl.BlockSpec(memory_space=pl.ANY)
    grid_spec = pltpu.PrefetchScalarGridSpec(
        num_scalar_prefetch=3, grid=(n_slots // te,),
        in_specs=[pl.BlockSpec((te * TILE_ROWS, LANES), lambda b, be, nu, nx: (jnp.minimum(b, nu[0] - 1), 0)),
                  bspec(D_FF), bspec(D_FF), bspec(D_MODEL), hbm, hbm, hbm],
        out_specs=pl.BlockSpec((te * TILE_ROWS, LANES), lambda b, be, nu, nx: (b, 0)),
        scratch_shapes=[pltpu.VMEM((2, 3, D_MODEL, D_FF), F32),
                        pltpu.VMEM((D_MODEL, D_FF), BF16), pltpu.VMEM((D_MODEL, D_FF), BF16),
                        pltpu.VMEM((D_FF, D_MODEL), BF16), pltpu.SMEM((1,), jnp.int32),
                        pltpu.SemaphoreType.DMA((2,))])
    return pl.pallas_call(
        functools.partial(_ffn_kernel, te=te), grid_spec=grid_spec,
        out_shape=jax.ShapeDtypeStruct((n_slots * TILE_ROWS, LANES), F32),
        compiler_params=pltpu.CompilerParams(dimension_semantics=("arbitrary",), vmem_limit_bytes=FFN_VMEM_LIMIT),
        name="ffn",
    )(block_e, n_used, next_e, xs, b_gate[:, None, :], b_up[:, None, :], b_down[:, None, :], w_gate, w_up, w_down)


def _combine_kernel(dest_ref, h1_ref, gate_ref, ys_ref, o_ref, buf, sem, *, tc, n):
    i = pl.program_id(0)

    def gather(step, slot):
        base = step * tc

        def issue(tb, carry):
            for u in range(ROW_DMA_UNROLL):
                t = tb * ROW_DMA_UNROLL + u
                for k in range(TOP_K):
                    _token_copy(ys_ref, dest_ref[k * n + base + t], buf.at[slot, k], t,
                                sem.at[slot]).start(priority=k % 2)
            return carry

        lax.fori_loop(0, tc // ROW_DMA_UNROLL, issue, 0)

    @pl.when(i == 0)
    def _():
        gather(0, 0)

    slot = i % 2

    @pl.when(i + 1 < pl.num_programs(0))
    def _():
        gather(i + 1, 1 - slot)

    def drain(tb, carry):
        for _u in range(ROW_DMA_UNROLL):
            for k in range(TOP_K):
                _token_copy(ys_ref, 0, buf.at[slot, k], 0, sem.at[slot]).wait()
        return carry

    lax.fori_loop(0, tc // ROW_DMA_UNROLL, drain, 0)
    gate = gate_ref[...]
    out = h1_ref[...]
    for k in range(TOP_K):
        out = out + gate[:, k:k + 1] * _load_token_tiles(buf.at[slot, k], tc)
    o_ref[...] = out


def _combine_call(dest_flat, h1, gate, ys, tc):
    n = h1.shape[0]
    grid_spec = pltpu.PrefetchScalarGridSpec(
        num_scalar_prefetch=1, grid=(n // tc,),
        in_specs=[pl.BlockSpec((tc, D_MODEL), lambda i, d: (i, 0)),
                  pl.BlockSpec((tc, LANES), lambda i, d: (i, 0)),
                  pl.BlockSpec(memory_space=pl.ANY)],
        out_specs=pl.BlockSpec((tc, D_MODEL), lambda i, d: (i, 0)),
        scratch_shapes=[pltpu.VMEM((2, TOP_K, tc * TILE_ROWS, LANES), F32), pltpu.SemaphoreType.DMA((2,))])
    return pl.pallas_call(
        functools.partial(_combine_kernel, tc=tc, n=n), grid_spec=grid_spec,
        out_shape=jax.ShapeDtypeStruct((n, D_MODEL), F32),
        compiler_params=pltpu.CompilerParams(dimension_semantics=("arbitrary",), vmem_limit_bytes=VMEM_LIMIT),
        name="combine",
    )(dest_flat, h1, gate, ys)


def _prepare_weights(attn_norm_g, w_in, swa_q_norm_g, swa_k_norm_g, mla_q_lat_norm_g, w_mla_q_up,
                     mla_kv_lat_norm_g, w_mla_kv_up, mla_q_norm_g, mla_k_norm_g, w_out, ffn_norm_g,
                     router_w, router_b):
    o_k = SWA_WIDTH
    o_v = o_k + SWA_KV_WIDTH
    o_ql = o_v + SWA_KV_WIDTH
    o_kv = o_ql + MLA_Q_LORA
    o_kr = o_kv + MLA_KV_LORA
    perm = np.arange(SWA_HEADS).reshape(SWA_KV_HEADS, SWA_GROUP).T.reshape(-1)
    w_qs = w_in[:, :o_k].reshape(D_MODEL, SWA_HEADS, SWA_HEAD_DIM)[:, perm].reshape(D_MODEL, SWA_WIDTH)
    w_kr = jnp.zeros((D_MODEL, LANES), F32).at[:, MLA_NOPE_DIM:MLA_QK_DIM].set(w_in[:, o_kr:])
    half = MLA_ROPE_DIM // 2
    r0, r1, r2 = MLA_NOPE_DIM, MLA_NOPE_DIM + half, MLA_QK_DIM

    def partner_lanes(a):
        z = jnp.zeros_like(a)
        return z.at[..., r0:r1].set(a[..., r1:r2]).at[..., r1:r2].set(a[..., r0:r1])

    w_in_r = jnp.concatenate([w_qs, w_in[:, o_k:o_kr], w_kr, partner_lanes(w_kr)], axis=1).astype(BF16)

    pad_qk = LANES - MLA_QK_DIM
    w_qup = jnp.pad(w_mla_q_up.reshape(MLA_Q_LORA, MLA_HEADS, MLA_QK_DIM), ((0, 0), (0, 0), (0, pad_qk)))
    w_qup = jnp.concatenate([w_qup.reshape(MLA_Q_LORA, MLA_HEADS * LANES),
                             partner_lanes(w_qup).reshape(MLA_Q_LORA, MLA_HEADS * LANES)], axis=1).astype(BF16)
    kv_up = w_mla_kv_up.reshape(MLA_KV_LORA, MLA_HEADS, MLA_NOPE_DIM + MLA_V_DIM)
    w_k = jnp.pad(kv_up[:, :, :MLA_NOPE_DIM], ((0, 0), (0, 0), (0, LANES - MLA_NOPE_DIM)))
    w_k = w_k.reshape(MLA_KV_LORA, MLA_HEADS * LANES).astype(BF16)
    w_vt = jnp.transpose(kv_up[:, :, MLA_NOPE_DIM:], (1, 2, 0))
    w_vt = jnp.pad(w_vt, ((0, 0), (0, VT_ROWS - MLA_V_DIM), (0, 0))).reshape(MLA_HEADS * VT_ROWS, MLA_KV_LORA)
    ones = np.zeros((MLA_HEADS, VT_ROWS, 1), np.float32)
    ones[:, MLA_V_DIM, 0] = 1.0

    w_oa = w_out[:SWA_WIDTH].reshape(SWA_HEADS, SWA_HEAD_DIM, D_MODEL)[perm].reshape(SWA_WIDTH, D_MODEL)
    rw = jnp.pad(router_w, ((0, 0), (0, LANES - N_EXPERTS)))
    rw_hi = rw.astype(BF16)
    rw_lo = (rw - rw_hi.astype(F32)).astype(BF16)
    rb = jnp.concatenate([router_b.astype(F32), jnp.full((LANES - N_EXPERTS,), NEG_BIG, F32)])[None, :]
    return {
        "g_attn": attn_norm_g[None, :], "w_in": w_in_r,
        "g_qs": jnp.tile(swa_q_norm_g, 2)[None, :] * (SWA_HEAD_DIM ** -0.5),
        "g_ks": jnp.tile(swa_k_norm_g, 2)[None, :],
        "g_qlat": mla_q_lat_norm_g[None, :], "w_qup": w_qup,
        "g_kvlat": mla_kv_lat_norm_g[None, :], "w_k": w_k, "w_vt": w_vt.astype(BF16),
        "ones": jnp.asarray(ones.reshape(MLA_HEADS * VT_ROWS, 1)),
        "g_qm": jnp.pad(mla_q_norm_g, (0, pad_qk))[None, :] * (MLA_QK_DIM ** -0.5 * LOG2_E),
        "g_km": jnp.pad(mla_k_norm_g, (0, pad_qk))[None, :],
        "g_qm_rot": partner_lanes(jnp.pad(mla_q_norm_g, (0, pad_qk))[None, :] * (MLA_QK_DIM ** -0.5 * LOG2_E)),
        "g_km_rot": partner_lanes(jnp.pad(mla_k_norm_g, (0, pad_qk))[None, :]),
        "w_oa": w_oa.astype(BF16), "w_ob": w_out[SWA_WIDTH:].astype(BF16),
        "g_ffn": ffn_norm_g[None, :], "rw_hi": rw_hi, "rw_lo": rw_lo, "rb": rb,
    }


def _rope_tables(n_rows):
    half = MLA_ROPE_DIM // 2
    inv = (1.0 / (ROPE_THETA ** (np.arange(half, dtype=np.float32) / half))).astype(np.float32)
    ang = np.arange(n_rows, dtype=np.float32)[:, None] * inv[None, :]
    cos, sin = jnp.asarray(np.cos(ang)), jnp.asarray(np.sin(ang))
    one = jnp.ones((n_rows, MLA_NOPE_DIM), F32)
    zero = jnp.zeros((n_rows, MLA_NOPE_DIM), F32)
    pad1 = jnp.ones((n_rows, LANES - MLA_QK_DIM), F32)
    pad0 = jnp.zeros((n_rows, LANES - MLA_QK_DIM), F32)
    return (jnp.concatenate([one, cos, cos, pad1], axis=1),
            jnp.concatenate([zero, -sin, sin, pad0], axis=1))


def kernel(x, meta_tokens, attn_norm_g, w_in, swa_q_norm_g, swa_k_norm_g, swa_sink, mla_q_lat_norm_g, w_mla_q_up, mla_kv_lat_norm_g, w_mla_kv_up, mla_q_norm_g, mla_k_norm_g, w_out, ffn_norm_g, router_w, router_b, w_gate, b_gate, w_up, b_up, w_down, b_down):
    bsz, n, d = x.shape
    assert bsz == 1 and d == D_MODEL and n % TK_MLA == 0 and attn_norm_g.shape[0] == 1
    wts = _prepare_weights(attn_norm_g[0], w_in[0], swa_q_norm_g[0], swa_k_norm_g[0], mla_q_lat_norm_g[0],
                           w_mla_q_up[0], mla_kv_lat_norm_g[0], w_mla_kv_up[0], mla_q_norm_g[0],
                           mla_k_norm_g[0], w_out[0], ffn_norm_g[0], router_w[0], router_b[0])
    xr = x[0]
    cos_t, sin_t = _rope_tables(N_META + n)

    xm = jnp.pad(meta_tokens.astype(F32), ((0, LANES - N_META), (0, 0)))
    valid_m = (jnp.arange(LANES) < N_META).astype(F32)[None, :]
    _, ks_m, vs_m, _, km_m, vt_m = _proj_call(xm, valid_m, cos_t[:LANES], sin_t[:LANES], wts, LANES)
    qs, ks, vs, qm, km, vt = _proj_call(xr, jnp.ones((1, n), F32), cos_t[N_META:], sin_t[N_META:], wts, TM_PROJ)

    ya = _swa_call(swa_sink[0], qs, ks, vs, ks_m, vs_m)
    yt = _mla_call(qm, km, vt, km_m, vt_m, TQ_MLA, TK_MLA)
    h1, m, ids, rank, gate, cnt = _post_call(xr, ya, yt, wts, TM_POST)

    te = TE_FFN
    counts = cnt[0, :N_EXPERTS].astype(jnp.int32)
    padded = (counts + te - 1) // te * te
    pad_end = jnp.cumsum(padded)
    pad_start = pad_end - padded
    expert = jnp.arange(N_EXPERTS, dtype=jnp.int32)[:, None, None]
    start_of = jnp.sum(jnp.where(ids[None, :TOP_K] == expert, pad_start[:, None, None], 0), axis=0)
    dest = (start_of + rank[:TOP_K]).reshape(-1)
    n_blocks = (n * TOP_K + N_EXPERTS * (te - 1)) // te
    block_start = jnp.arange(n_blocks, dtype=jnp.int32) * te
    block_e = jnp.minimum(jnp.sum((pad_end[None, :] <= block_start[:, None]).astype(jnp.int32), axis=1),
                          N_EXPERTS - 1)
    n_used = (pad_end[-1:] // te).astype(jnp.int32)
    tail_start = jnp.where(padded > 0, pad_end - te, -1).astype(jnp.int32)
    eid = jnp.arange(N_EXPERTS, dtype=jnp.int32)
    later_used = (padded[None, :] > 0) & (eid[None, :] > eid[:, None])
    next_e = jnp.min(jnp.where(later_used, eid[None, :], N_EXPERTS), axis=1)
    next_e = jnp.where(next_e < N_EXPERTS, next_e, -1).astype(jnp.int32)

    xs = _dispatch_call(dest, tail_start, n_used, m, n_blocks * te, TD_DISPATCH, te)
    ys = _ffn_call(block_e, n_used, next_e, xs, w_gate[0], b_gate[0], w_up[0], b_up[0], w_down[0], b_down[0], te)
    out = _combine_call(dest, h1, gate, ys, TC_COMBINE)
    return out[None]
```

```python
import functools

import numpy as np
import jax
import jax.numpy as jnp
from jax import lax
from jax.experimental import pallas as pl
from jax.experimental.pallas import tpu as pltpu

F32 = jnp.float32
BF16 = jnp.bfloat16

D_MODEL = 1024
N_META = 16
BLOCK = 128
WINDOW = 128
NORM_EPS = 1e-6

SWA_HEADS = 8
SWA_KV_HEADS = 2
SWA_GROUP = SWA_HEADS // SWA_KV_HEADS
SWA_HEAD_DIM = 64
SWA_WIDTH = SWA_HEADS * SWA_HEAD_DIM
SWA_KV_WIDTH = SWA_KV_HEADS * SWA_HEAD_DIM

MLA_HEADS = 8
MLA_Q_LORA = 256
MLA_KV_LORA = 128
MLA_NOPE_DIM = 64
MLA_ROPE_DIM = 32
MLA_V_DIM = 64
MLA_QK_DIM = MLA_NOPE_DIM + MLA_ROPE_DIM
MLA_WIDTH = MLA_HEADS * MLA_V_DIM
ROPE_THETA = 10000.0

N_EXPERTS = 32
TOP_K = 4
D_FF = 1024
SWIGLU_LIMIT = 7.0
SWIGLU_ALPHA = 1.702

LANES = 128
VT_ROWS = 80
NEG_BIG = -1e30
LOG2_E = 1.4426950408889634
VMEM_LIMIT = 48 * 1024 * 1024
FFN_VMEM_LIMIT = 56 * 1024 * 1024

TM_PROJ = 512
TQ_MLA = 512
TK_MLA = 512
MLA_UNROLL = 16
ROW_DMA_UNROLL = 8
TM_POST = 512
TD_DISPATCH = 256
TE_FFN = 512
TC_COMBINE = 256

_NT = (((1,), (1,)), ((), ()))


def _rms(v, n):
    return v * lax.rsqrt(jnp.sum(v * v, axis=-1, keepdims=True) * (1.0 / n) + NORM_EPS)


TILE_ROWS = D_MODEL // LANES


def _store_token_tiles(ref, v):
    rows = v.shape[0]
    for s in range(TILE_ROWS):
        ref[pl.ds(s, rows, stride=TILE_ROWS), :] = v[:, s * LANES:(s + 1) * LANES]


def _load_token_tiles(ref, rows):
    return jnp.concatenate([ref[pl.ds(s, rows, stride=TILE_ROWS), :] for s in range(TILE_ROWS)], axis=1)


def _proj_kernel(x_ref, valid_ref, cos_ref, sin_ref, g_attn_ref, w_in_ref, g_qs_ref, g_ks_ref,
                 g_qlat_ref, w_qup_ref, g_kvlat_ref, w_k_ref, w_vt_ref, ones_ref, g_qm_ref, g_km_ref,
                 g_qm_rot_ref, g_km_rot_ref,
                 qs_ref, ks_ref, vs_ref, qm_ref, km_ref, vt_ref):
    tm = x_ref.shape[0]
    a = _rms(x_ref[...], D_MODEL) * g_attn_ref[...]
    proj = jnp.dot(a.astype(BF16), w_in_ref[...], preferred_element_type=F32)

    lane = lax.broadcasted_iota(jnp.int32, (tm, LANES), 1)
    lo = lane < SWA_HEAD_DIM

    def seg_norm(v, g):
        v2 = v * v
        s_all = jnp.sum(v2, axis=-1, keepdims=True)
        s_lo = jnp.sum(jnp.where(lo, v2, 0.0), axis=-1, keepdims=True)
        ms = jnp.where(lo, s_lo, s_all - s_lo) * (1.0 / SWA_HEAD_DIM)
        return v * lax.rsqrt(ms + NORM_EPS) * g

    g_qs = g_qs_ref[...]
    for j in range(SWA_WIDTH // LANES):
        sl = slice(j * LANES, (j + 1) * LANES)
        qs_ref[:, sl] = seg_norm(proj[:, sl], g_qs).astype(BF16)
    o_k = SWA_WIDTH
    o_v = o_k + SWA_KV_WIDTH
    o_ql = o_v + SWA_KV_WIDTH
    o_kv = o_ql + MLA_Q_LORA
    o_kr = o_kv + MLA_KV_LORA
    ks_ref[...] = seg_norm(proj[:, o_k:o_v], g_ks_ref[...]).astype(BF16)
    vs_ref[...] = proj[:, o_v:o_ql].astype(BF16)

    cosv = cos_ref[...]
    sinv = sin_ref[...]

    def norm_rope(v, v_rot, g, g_rot):
        ms = jnp.sum(v * v, axis=-1, keepdims=True) * (1.0 / MLA_QK_DIM)
        return (v * g * cosv + v_rot * g_rot * sinv) * lax.rsqrt(ms + NORM_EPS)

    hw = MLA_HEADS * LANES
    qln = _rms(proj[:, o_ql:o_kv], MLA_Q_LORA) * g_qlat_ref[...]
    qup = jnp.dot(qln.astype(BF16), w_qup_ref[...], preferred_element_type=F32)
    g_qm = g_qm_ref[...]
    g_qm_rot = g_qm_rot_ref[...]
    for h in range(MLA_HEADS):
        sl = slice(h * LANES, (h + 1) * LANES)
        sl_rot = slice(hw + h * LANES, hw + (h + 1) * LANES)
        qm_ref[h] = norm_rope(qup[:, sl], qup[:, sl_rot], g_qm, g_qm_rot).T.astype(BF16)

    kvn = (_rms(proj[:, o_kv:o_kr], MLA_KV_LORA) * g_kvlat_ref[...]).astype(BF16)
    knope = jnp.dot(kvn, w_k_ref[...], preferred_element_type=F32)
    krope = proj[:, o_kr:o_kr + LANES]
    krope_rot = proj[:, o_kr + LANES:o_kr + 2 * LANES]
    g_km = g_km_ref[...]
    g_km_rot = g_km_rot_ref[...]
    for h in range(MLA_HEADS):
        km_ref[h] = norm_rope(knope[:, h * LANES:(h + 1) * LANES] + krope, krope_rot, g_km, g_km_rot).astype(BF16)

    vt = lax.dot_general(w_vt_ref[...], kvn, _NT, preferred_element_type=F32)
    vt = vt + ones_ref[...] * valid_ref[...]
    for h in range(MLA_HEADS):
        vt_ref[h] = vt[h * VT_ROWS:(h + 1) * VT_ROWS].astype(BF16)


def _proj_call(x, valid, cos_t, sin_t, wts, tm):
    n = x.shape[0]
    w_names = ("g_attn", "w_in", "g_qs", "g_ks", "g_qlat", "w_qup", "g_kvlat", "w_k", "w_vt", "ones", "g_qm", "g_km",
               "g_qm_rot", "g_km_rot")
    w_args = [wts[k] for k in w_names]

    def full(a):
        return pl.BlockSpec(a.shape, lambda i: (0,) * a.ndim)

    in_specs = [pl.BlockSpec((tm, D_MODEL), lambda i: (i, 0)),
                pl.BlockSpec((1, tm), lambda i: (0, i)),
                pl.BlockSpec((tm, LANES), lambda i: (i, 0)),
                pl.BlockSpec((tm, LANES), lambda i: (i, 0))] + [full(a) for a in w_args]
    out_shape = (jax.ShapeDtypeStruct((n, SWA_WIDTH), BF16),
                 jax.ShapeDtypeStruct((n, SWA_KV_WIDTH), BF16),
                 jax.ShapeDtypeStruct((n, SWA_KV_WIDTH), BF16),
                 jax.ShapeDtypeStruct((MLA_HEADS, LANES, n), BF16),
                 jax.ShapeDtypeStruct((MLA_HEADS, n, LANES), BF16),
                 jax.ShapeDtypeStruct((MLA_HEADS, VT_ROWS, n), BF16))
    out_specs = (pl.BlockSpec((tm, SWA_WIDTH), lambda i: (i, 0)),
                 pl.BlockSpec((tm, SWA_KV_WIDTH), lambda i: (i, 0)),
                 pl.BlockSpec((tm, SWA_KV_WIDTH), lambda i: (i, 0)),
                 pl.BlockSpec((MLA_HEADS, LANES, tm), lambda i: (0, 0, i)),
                 pl.BlockSpec((MLA_HEADS, tm, LANES), lambda i: (0, i, 0)),
                 pl.BlockSpec((MLA_HEADS, VT_ROWS, tm), lambda i: (0, 0, i)))
    return pl.pallas_call(
        _proj_kernel, grid=(n // tm,), in_specs=in_specs, out_specs=out_specs, out_shape=out_shape,
        compiler_params=pltpu.CompilerParams(dimension_semantics=("arbitrary",), vmem_limit_bytes=VMEM_LIMIT),
        name="proj",
    )(x, valid, cos_t, sin_t, *w_args)


def _swa_bias_table():
    nkeys = 3 * BLOCK + LANES
    r = np.arange(BLOCK)[:, None]
    c = np.arange(nkeys)[None, :]
    d = np.abs(BLOCK + r - c)
    in_win = (d <= WINDOW) & (c < 3 * BLOCK)
    is_meta = (c >= 3 * BLOCK) & (c < 3 * BLOCK + N_META)
    slopes = 2.0 ** (-8.0 * np.arange(1, SWA_HEADS + 1) / SWA_HEADS)
    bias = np.where(in_win[None], -slopes[:, None, None] * d[None], np.where(is_meta[None], 0.0, NEG_BIG))
    return jnp.asarray(bias, F32)


def _swa_kernel(sink_ref, q_ref, kp_ref, kc_ref, kn_ref, vp_ref, vc_ref, vn_ref, km_ref, vm_ref, bias_ref, o_ref, s_scr):
    b = pl.program_id(0)
    nb = pl.num_programs(0)
    nkeys = 3 * BLOCK + LANES
    kall = jnp.concatenate([kp_ref[...], kc_ref[...], kn_ref[...], km_ref[...]], axis=0)
    vall = jnp.concatenate([vp_ref[...], vc_ref[...], vn_ref[...], vm_ref[...]], axis=0)

    c = lax.broadcasted_iota(jnp.int32, (1, nkeys), 1)
    lo_edge = jnp.where(b > 0, 0, BLOCK)
    hi_edge = jnp.where(b < nb - 1, 3 * BLOCK, 2 * BLOCK)
    edge = jnp.where((c < lo_edge) | ((c >= hi_edge) & (c < 3 * BLOCK)), NEG_BIG, 0.0)

    lane = lax.broadcasted_iota(jnp.int32, (BLOCK, LANES), 1)
    lo = lane < SWA_HEAD_DIM
    for j in range(SWA_GROUP):
        qg = q_ref[:, j * LANES:(j + 1) * LANES].astype(F32)
        q2 = jnp.concatenate([jnp.where(lo, qg, 0.0), jnp.where(lo, 0.0, qg)], axis=0).astype(BF16)
        s_scr[j] = lax.dot_general(q2, kall, _NT, preferred_element_type=F32)
    for j in range(SWA_GROUP):
        outs = []
        for half, h in ((0, j), (1, j + SWA_GROUP)):
            sh = s_scr[j, half * BLOCK:(half + 1) * BLOCK] + bias_ref[h] + edge
            sink = sink_ref[h]
            m = jnp.maximum(jnp.max(sh, axis=-1, keepdims=True), sink)
            p = jnp.exp(sh - m)
            den = jnp.sum(p, axis=-1, keepdims=True) + jnp.exp(sink - m)
            o = jnp.dot(p.astype(BF16), vall, preferred_element_type=F32)
            outs.append(o / den)
        o_ref[:, j * LANES:(j + 1) * LANES] = jnp.where(lo, outs[0], outs[1]).astype(BF16)


def _swa_call(sink, qs, ks, vs, ks_meta, vs_meta):
    n = qs.shape[0]
    nb = n // BLOCK
    bias = _swa_bias_table()
    kv_prev = pl.BlockSpec((BLOCK, SWA_KV_WIDTH), lambda b: (jnp.maximum(b - 1, 0), 0))
    kv_cur = pl.BlockSpec((BLOCK, SWA_KV_WIDTH), lambda b: (b, 0))
    kv_next = pl.BlockSpec((BLOCK, SWA_KV_WIDTH), lambda b: (jnp.minimum(b + 1, nb - 1), 0))
    meta = pl.BlockSpec((LANES, SWA_KV_WIDTH), lambda b: (0, 0))
    return pl.pallas_call(
        _swa_kernel,
        grid=(nb,),
        in_specs=[pl.BlockSpec(memory_space=pltpu.SMEM),
                  pl.BlockSpec((BLOCK, SWA_WIDTH), lambda b: (b, 0)),
                  kv_prev, kv_cur, kv_next, kv_prev, kv_cur, kv_next, meta, meta,
                  pl.BlockSpec(bias.shape, lambda b: (0, 0, 0))],
        out_specs=pl.BlockSpec((BLOCK, SWA_WIDTH), lambda b: (b, 0)),
        out_shape=jax.ShapeDtypeStruct((n, SWA_WIDTH), BF16),
        scratch_shapes=[pltpu.VMEM((SWA_GROUP, 2 * BLOCK, 3 * BLOCK + LANES), F32)],
        compiler_params=pltpu.CompilerParams(dimension_semantics=("arbitrary",), vmem_limit_bytes=VMEM_LIMIT),
        name="swa",
    )(sink, qs, ks, ks, ks, vs, vs, vs, ks_meta, vs_meta, bias)


def _mla_kernel(q_ref, k_ref, vt_ref, km_ref, vtm_ref, o_ref, s_scr, *, tk, nk, unroll):
    qt = q_ref[0]
    tq = qt.shape[1]

    def scores(off):
        return jnp.dot(k_ref[0, pl.ds(off, tk), :], qt, preferred_element_type=F32)

    def soft_pv(s, vt, m, acc):
        m_new = jnp.maximum(m, jnp.max(s, axis=0, keepdims=True))
        alpha = jnp.exp2(m - m_new)
        p = jnp.exp2(s - m_new).astype(BF16)
        acc = alpha * acc + jnp.dot(vt, p, preferred_element_type=F32)
        return m_new, acc

    m0 = jnp.full((1, tq), NEG_BIG, F32)
    acc0 = jnp.zeros((VT_ROWS, tq), F32)
    s_meta = jnp.dot(km_ref[0], qt, preferred_element_type=F32)
    s_scr[0] = scores(0)
    m, acc = soft_pv(s_meta, vtm_ref[0], m0, acc0)

    def body(i, carry):
        m, acc = carry
        for u in range(unroll):
            t = i * unroll + u
            cur = u % 2
            s_next = scores(pl.multiple_of(jnp.minimum(t + 1, nk - 1) * tk, tk))
            off = pl.multiple_of(t * tk, tk)
            m, acc = soft_pv(s_scr[cur], vt_ref[0, :, pl.ds(off, tk)], m, acc)
            s_scr[1 - cur] = s_next
        return m, acc

    m, acc = lax.fori_loop(0, nk // unroll, body, (m, acc))
    o_ref[...] = (acc[0:MLA_V_DIM] / acc[MLA_V_DIM:MLA_V_DIM + 1]).astype(BF16)


def _mla_call(qm, km, vt, km_meta, vt_meta, tq, tk):
    n = qm.shape[2]
    nk = n // tk
    unroll = MLA_UNROLL if nk % MLA_UNROLL == 0 else 2
    assert nk % unroll == 0
    return pl.pallas_call(
        functools.partial(_mla_kernel, tk=tk, nk=nk, unroll=unroll),
        grid=(MLA_HEADS, n // tq),
        in_specs=[pl.BlockSpec((1, LANES, tq), lambda h, i: (h, 0, i)),
                  pl.BlockSpec((1, n, LANES), lambda h, i: (h, 0, 0)),
                  pl.BlockSpec((1, VT_ROWS, n), lambda h, i: (h, 0, 0)),
                  pl.BlockSpec((1, LANES, LANES), lambda h, i: (h, 0, 0)),
                  pl.BlockSpec((1, VT_ROWS, LANES), lambda h, i: (h, 0, 0))],
        out_specs=pl.BlockSpec((MLA_V_DIM, tq), lambda h, i: (h, i)),
        out_shape=jax.ShapeDtypeStruct((MLA_WIDTH, n), BF16),
        scratch_shapes=[pltpu.VMEM((2, tk, tq), F32)],
        compiler_params=pltpu.CompilerParams(dimension_semantics=("arbitrary", "arbitrary"),
                                             vmem_limit_bytes=VMEM_LIMIT),
        name="mla",
    )(qm, km, vt, km_meta, vt_meta)


def _post_kernel(x_ref, ya_ref, yt_ref, w_oa_ref, w_ob_ref, g_ffn_ref, rw_hi_ref, rw_lo_ref, rb_ref,
                 h1_ref, m_ref, ids_ref, rank_ref, gate_ref, cnt_ref, base_ref):
    tm = x_ref.shape[0]

    @pl.when(pl.program_id(0) == 0)
    def _():
        base_ref[...] = jnp.zeros_like(base_ref)

    yb = yt_ref[...].T
    h1 = (x_ref[...]
          + jnp.dot(ya_ref[...], w_oa_ref[...], preferred_element_type=F32)
          + jnp.dot(yb, w_ob_ref[...], preferred_element_type=F32))
    h1_ref[...] = h1
    m = _rms(h1, D_MODEL) * g_ffn_ref[...]
    _store_token_tiles(m_ref, m)

    m_hi = m.astype(BF16)
    m_lo = (m - m_hi.astype(F32)).astype(BF16)
    rw_hi = rw_hi_ref[...]
    logits = (jnp.dot(m_hi, rw_hi, preferred_element_type=F32)
              + jnp.dot(m_lo, rw_hi, preferred_element_type=F32)
              + jnp.dot(m_hi, rw_lo_ref[...], preferred_element_type=F32)
              + rb_ref[...])

    lane = lax.broadcasted_iota(jnp.int32, (tm, LANES), 1)
    lane_f = lane.astype(F32)
    work = logits
    vals, idxs = [], []
    for _k in range(TOP_K):
        mx = jnp.max(work, axis=-1, keepdims=True)
        ix = jnp.min(jnp.where(work == mx, lane_f, float(LANES)), axis=-1, keepdims=True)
        vals.append(mx)
        idxs.append(ix)
        work = jnp.where(lane_f == ix, -3e38, work)
    exps = [jnp.exp(v - vals[0]) for v in vals]
    den = exps[0] + exps[1] + exps[2] + exps[3]

    hits = [lane_f == ix for ix in idxs]
    onehot = jnp.zeros((tm, LANES), F32)
    for hk in hits:
        onehot = onehot + jnp.where(hk, 1.0, 0.0)
    row = lax.broadcasted_iota(jnp.int32, (tm, tm), 0)
    col = lax.broadcasted_iota(jnp.int32, (tm, tm), 1)
    tri = jnp.where(row > col, 1.0, 0.0).astype(BF16)
    before = jnp.dot(tri, onehot.astype(BF16), preferred_element_type=F32) + base_ref[...]
    ids = jnp.zeros((tm, LANES), F32)
    rank = jnp.zeros((tm, LANES), F32)
    gate = jnp.zeros((tm, LANES), F32)
    for k in range(TOP_K):
        rk = jnp.sum(jnp.where(hits[k], before, 0.0), axis=-1, keepdims=True)
        sel = lane == k
        ids = jnp.where(sel, idxs[k], ids)
        rank = jnp.where(sel, rk, rank)
        gate = jnp.where(sel, exps[k] / den, gate)
    ids_ref[...] = ids.T[0:8].astype(jnp.int32)
    rank_ref[...] = rank.T[0:8].astype(jnp.int32)
    gate_ref[...] = gate
    base = base_ref[...] + jnp.sum(onehot, axis=0, keepdims=True)
    base_ref[...] = base
    cnt_ref[...] = base


def _post_call(x, ya, yt, wts, tm):
    n = x.shape[0]
    w_names = ("w_oa", "w_ob", "g_ffn", "rw_hi", "rw_lo", "rb")
    w_args = [wts[k] for k in w_names]

    def full(a):
        return pl.BlockSpec(a.shape, lambda i: (0,) * a.ndim)

    row = lambda w: pl.BlockSpec((tm, w), lambda i: (i, 0))
    choice = pl.BlockSpec((8, tm), lambda i: (0, i))
    return pl.pallas_call(
        _post_kernel, grid=(n // tm,),
        in_specs=[row(D_MODEL), row(SWA_WIDTH), pl.BlockSpec((MLA_WIDTH, tm), lambda i: (0, i))]
        + [full(a) for a in w_args],
        out_specs=(row(D_MODEL), pl.BlockSpec((tm * TILE_ROWS, LANES), lambda i: (i, 0)), choice, choice, row(LANES),
                   pl.BlockSpec((1, LANES), lambda i: (0, 0))),
        out_shape=(jax.ShapeDtypeStruct((n, D_MODEL), F32), jax.ShapeDtypeStruct((n * TILE_ROWS, LANES), F32),
                   jax.ShapeDtypeStruct((8, n), jnp.int32), jax.ShapeDtypeStruct((8, n), jnp.int32),
                   jax.ShapeDtypeStruct((n, LANES), F32), jax.ShapeDtypeStruct((1, LANES), F32)),
        scratch_shapes=[pltpu.VMEM((1, LANES), F32)],
        compiler_params=pltpu.CompilerParams(dimension_semantics=("arbitrary",), vmem_limit_bytes=VMEM_LIMIT),
        name="post",
    )(x, ya, yt, *w_args)


def _token_copy(src, src_tok, dst, dst_tok, sem):
    rows = lambda tok: pl.ds(pl.multiple_of(tok * TILE_ROWS, TILE_ROWS), TILE_ROWS)
    return pltpu.make_async_copy(src.at[rows(src_tok), :], dst.at[rows(dst_tok), :], sem)


def _dispatch_kernel(dest_ref, tail_ref, nused_ref, m_ref, xs_ref, zbuf, sem, zsem, *, td, n, te):
    block_rows = te * TILE_ROWS

    def zero_copy(slot):
        start = pl.multiple_of(slot * TILE_ROWS, block_rows)
        return pltpu.make_async_copy(zbuf, xs_ref.at[pl.ds(start, block_rows), :], zsem)

    @pl.when(pl.program_id(0) == 0)
    def _():
        zbuf[...] = jnp.zeros_like(zbuf)
        n_blocks = xs_ref.shape[0] // block_rows
        for e in range(N_EXPERTS):
            @pl.when(tail_ref[e] >= 0)
            def _():
                zero_copy(jnp.maximum(tail_ref[e], 0)).start()

        def start_unused(b, carry):
            zero_copy(b * te).start()
            return carry

        lax.fori_loop(nused_ref[0], n_blocks, start_unused, 0)
        for e in range(N_EXPERTS):
            @pl.when(tail_ref[e] >= 0)
            def _():
                zero_copy(0).wait()

        def wait_unused(b, carry):
            zero_copy(0).wait()
            return carry

        lax.fori_loop(nused_ref[0], n_blocks, wait_unused, 0)

    base = pl.program_id(0) * td

    def issue(tb, carry):
        for u in range(ROW_DMA_UNROLL):
            t = tb * ROW_DMA_UNROLL + u
            for k in range(TOP_K):
                _token_copy(m_ref, t, xs_ref, dest_ref[k * n + base + t], sem).start(priority=k % 2)
        return carry

    lax.fori_loop(0, td // ROW_DMA_UNROLL, issue, 0)

    def drain(tb, carry):
        for _u in range(ROW_DMA_UNROLL * TOP_K):
            _token_copy(m_ref, 0, xs_ref, 0, sem).wait()
        return carry

    lax.fori_loop(0, td // ROW_DMA_UNROLL, drain, 0)


def _dispatch_call(dest_flat, tail_start, n_used, m, n_slots, td, te):
    n = m.shape[0] // TILE_ROWS
    grid_spec = pltpu.PrefetchScalarGridSpec(
        num_scalar_prefetch=3, grid=(n // td,),
        in_specs=[pl.BlockSpec((td * TILE_ROWS, LANES), lambda i, d, z, u: (i, 0))],
        out_specs=pl.BlockSpec(memory_space=pl.ANY),
        scratch_shapes=[pltpu.VMEM((te * TILE_ROWS, LANES), F32), pltpu.SemaphoreType.DMA,
                        pltpu.SemaphoreType.DMA])
    return pl.pallas_call(
        functools.partial(_dispatch_kernel, td=td, n=n, te=te), grid_spec=grid_spec,
        out_shape=jax.ShapeDtypeStruct((n_slots * TILE_ROWS, LANES), F32),
        compiler_params=pltpu.CompilerParams(dimension_semantics=("arbitrary",), vmem_limit_bytes=VMEM_LIMIT),
        name="dispatch",
    )(dest_flat, tail_start, n_used, m)


def _ffn_kernel(be_ref, nused_ref, next_ref, xs_ref, bg_ref, bu_ref, bd_ref, wg_hbm, wu_hbm, wd_hbm, ys_ref,
                wbuf, wg_s, wu_s, wd_s, slot_ref, wsem, *, te):
    b = pl.program_id(0)
    used = b < nused_ref[0]
    e = be_ref[b]
    fresh = (b == 0) | (e != be_ref[jnp.maximum(b - 1, 0)])

    def weight_copies(expert, slot):
        return [pltpu.make_async_copy(w.at[expert], wbuf.at[slot, j], wsem.at[slot])
                for j, w in enumerate((wg_hbm, wu_hbm, wd_hbm))]

    @pl.when(b == 0)
    def _():
        slot_ref[0] = 0
        for c in weight_copies(e, 0):
            c.start()

    @pl.when(fresh & used)
    def _():
        slot = slot_ref[0]
        for c in weight_copies(e, slot):
            c.wait()
        nxt = next_ref[e]

        @pl.when(nxt >= 0)
        def _():
            for c in weight_copies(jnp.maximum(nxt, 0), 1 - slot):
                c.start()

        wg_s[...] = wbuf[slot, 0].astype(BF16)
        wu_s[...] = wbuf[slot, 1].astype(BF16)
        wd_s[...] = wbuf[slot, 2].astype(BF16)
        slot_ref[0] = 1 - slot

    @pl.when(used)
    def _():
        x = _load_token_tiles(xs_ref, te).astype(BF16)
        g = jnp.dot(x, wg_s[...], preferred_element_type=F32) + bg_ref[0]
        u = jnp.dot(x, wu_s[...], preferred_element_type=F32) + bu_ref[0]
        g = jnp.minimum(g, SWIGLU_LIMIT)
        u = jnp.clip(u, -SWIGLU_LIMIT, SWIGLU_LIMIT)
        act = (u + 1.0) * (g * jax.nn.sigmoid(SWIGLU_ALPHA * g))
        y = jnp.dot(act.astype(BF16), wd_s[...], preferred_element_type=F32) + bd_ref[0]
        _store_token_tiles(ys_ref, y)

    @pl.when(jnp.logical_not(used))
    def _():
        ys_ref[...] = jnp.zeros_like(ys_ref)


def _ffn_call(block_e, n_used, next_e, xs, w_gate, b_gate, w_up, b_up, w_down, b_down, te):
    n_slots = xs.shape[0] // TILE_ROWS
    bspec = lambda c: pl.BlockSpec((1, 1, c), lambda b, be, nu, nx: (be[b], 0, 0))
    hbm = pl.BlockSpec(memory_space=pl.ANY)
    grid_spec = pltpu.PrefetchScalarGridSpec(
        num_scalar_prefetch=3, grid=(n_slots // te,),
        in_specs=[pl.BlockSpec((te * TILE_ROWS, LANES), lambda b, be, nu, nx: (jnp.minimum(b, nu[0] - 1), 0)),
                  bspec(D_FF), bspec(D_FF), bspec(D_MODEL), hbm, hbm, hbm],
        out_specs=pl.BlockSpec((te * TILE_ROWS, LANES), lambda b, be, nu, nx: (b, 0)),
        scratch_shapes=[pltpu.VMEM((2, 3, D_MODEL, D_FF), F32),
                        pltpu.VMEM((D_MODEL, D_FF), BF16), pltpu.VMEM((D_MODEL, D_FF), BF16),
                        pltpu.VMEM((D_FF, D_MODEL), BF16), pltpu.SMEM((1,), jnp.int32),
                        pltpu.SemaphoreType.DMA((2,))])
    return pl.pallas_call(
        functools.partial(_ffn_kernel, te=te), grid_spec=grid_spec,
        out_shape=jax.ShapeDtypeStruct((n_slots * TILE_ROWS, LANES), F32),
        compiler_params=pltpu.CompilerParams(dimension_semantics=("arbitrary",), vmem_limit_bytes=FFN_VMEM_LIMIT),
        name="ffn",
    )(block_e, n_used, next_e, xs, b_gate[:, None, :], b_up[:, None, :], b_down[:, None, :], w_gate, w_up, w_down)


def _combine_kernel(dest_ref, h1_ref, gate_ref, ys_ref, o_ref, buf, sem, *, tc, n):
    i = pl.program_id(0)

    def gather(step, slot):
        base = step * tc

        def issue(tb, carry):
            for u in range(ROW_DMA_UNROLL):
                t = tb * ROW_DMA_UNROLL + u
                for k in range(TOP_K):
                    _token_copy(ys_ref, dest_ref[k * n + base + t], buf.at[slot, k], t,
                                sem.at[slot]).start(priority=k % 2)
            return carry

        lax.fori_loop(0, tc // ROW_DMA_UNROLL, issue, 0)

    @pl.when(i == 0)
    def _():
        gather(0, 0)

    slot = i % 2

    @pl.when(i + 1 < pl.num_programs(0))
    def _():
        gather(i + 1, 1 - slot)

    def drain(tb, carry):
        for _u in range(ROW_DMA_UNROLL):
            for k in range(TOP_K):
                _token_copy(ys_ref, 0, buf.at[slot, k], 0, sem.at[slot]).wait()
        return carry

    lax.fori_loop(0, tc // ROW_DMA_UNROLL, drain, 0)
    gate = gate_ref[...]
    out = h1_ref[...]
    for k in range(TOP_K):
        out = out + gate[:, k:k + 1] * _load_token_tiles(buf.at[slot, k], tc)
    o_ref[...] = out


def _combine_call(dest_flat, h1, gate, ys, tc):
    n = h1.shape[0]
    grid_spec = pltpu.PrefetchScalarGridSpec(
        num_scalar_prefetch=1, grid=(n // tc,),
        in_specs=[pl.BlockSpec((tc, D_MODEL), lambda i, d: (i, 0)),
                  pl.BlockSpec((tc, LANES), lambda i, d: (i, 0)),
                  pl.BlockSpec(memory_space=pl.ANY)],
        out_specs=pl.BlockSpec((tc, D_MODEL), lambda i, d: (i, 0)),
        scratch_shapes=[pltpu.VMEM((2, TOP_K, tc * TILE_ROWS, LANES), F32), pltpu.SemaphoreType.DMA((2,))])
    return pl.pallas_call(
        functools.partial(_combine_kernel, tc=tc, n=n), grid_spec=grid_spec,
        out_shape=jax.ShapeDtypeStruct((n, D_MODEL), F32),
        compiler_params=pltpu.CompilerParams(dimension_semantics=("arbitrary",), vmem_limit_bytes=VMEM_LIMIT),
        name="combine",
    )(dest_flat, h1, gate, ys)


def _prepare_weights(attn_norm_g, w_in, swa_q_norm_g, swa_k_norm_g, mla_q_lat_norm_g, w_mla_q_up,
                     mla_kv_lat_norm_g, w_mla_kv_up, mla_q_norm_g, mla_k_norm_g, w_out, ffn_norm_g,
                     router_w, router_b):
    o_k = SWA_WIDTH
    o_v = o_k + SWA_KV_WIDTH
    o_ql = o_v + SWA_KV_WIDTH
    o_kv = o_ql + MLA_Q_LORA
    o_kr = o_kv + MLA_KV_LORA
    perm = np.arange(SWA_HEADS).reshape(SWA_KV_HEADS, SWA_GROUP).T.reshape(-1)
    w_qs = w_in[:, :o_k].reshape(D_MODEL, SWA_HEADS, SWA_HEAD_DIM)[:, perm].reshape(D_MODEL, SWA_WIDTH)
    w_kr = jnp.zeros((D_MODEL, LANES), F32).at[:, MLA_NOPE_DIM:MLA_QK_DIM].set(w_in[:, o_kr:])
    half = MLA_ROPE_DIM // 2
    r0, r1, r2 = MLA_NOPE_DIM, MLA_NOPE_DIM + half, MLA_QK_DIM

    def partner_lanes(a):
        z = jnp.zeros_like(a)
        return z.at[..., r0:r1].set(a[..., r1:r2]).at[..., r1:r2].set(a[..., r0:r1])

    w_in_r = jnp.concatenate([w_qs, w_in[:, o_k:o_kr], w_kr, partner_lanes(w_kr)], axis=1).astype(BF16)

    pad_qk = LANES - MLA_QK_DIM
    w_qup = jnp.pad(w_mla_q_up.reshape(MLA_Q_LORA, MLA_HEADS, MLA_QK_DIM), ((0, 0), (0, 0), (0, pad_qk)))
    w_qup = jnp.concatenate([w_qup.reshape(MLA_Q_LORA, MLA_HEADS * LANES),
                             partner_lanes(w_qup).reshape(MLA_Q_LORA, MLA_HEADS * LANES)], axis=1).astype(BF16)
    kv_up = w_mla_kv_up.reshape(MLA_KV_LORA, MLA_HEADS, MLA_NOPE_DIM + MLA_V_DIM)
    w_k = jnp.pad(kv_up[:, :, :MLA_NOPE_DIM], ((0, 0), (0, 0), (0, LANES - MLA_NOPE_DIM)))
    w_k = w_k.reshape(MLA_KV_LORA, MLA_HEADS * LANES).astype(BF16)
    w_vt = jnp.transpose(kv_up[:, :, MLA_NOPE_DIM:], (1, 2, 0))
    w_vt = jnp.pad(w_vt, ((0, 0), (0, VT_ROWS - MLA_V_DIM), (0, 0))).reshape(MLA_HEADS * VT_ROWS, MLA_KV_LORA)
    ones = np.zeros((MLA_HEADS, VT_ROWS, 1), np.float32)
    ones[:, MLA_V_DIM, 0] = 1.0

    w_oa = w_out[:SWA_WIDTH].reshape(SWA_HEADS, SWA_HEAD_DIM, D_MODEL)[perm].reshape(SWA_WIDTH, D_MODEL)
    rw = jnp.pad(router_w, ((0, 0), (0, LANES - N_EXPERTS)))
    rw_hi = rw.astype(BF16)
    rw_lo = (rw - rw_hi.astype(F32)).astype(BF16)
    rb = jnp.concatenate([router_b.astype(F32), jnp.full((LANES - N_EXPERTS,), NEG_BIG, F32)])[None, :]
    return {
        "g_attn": attn_norm_g[None, :], "w_in": w_in_r,
        "g_qs": jnp.tile(swa_q_norm_g, 2)[None, :] * (SWA_HEAD_DIM ** -0.5),
        "g_ks": jnp.tile(swa_k_norm_g, 2)[None, :],
        "g_qlat": mla_q_lat_norm_g[None, :], "w_qup": w_qup,
        "g_kvlat": mla_kv_lat_norm_g[None, :], "w_k": w_k, "w_vt": w_vt.astype(BF16),
        "ones": jnp.asarray(ones.reshape(MLA_HEADS * VT_ROWS, 1)),
        "g_qm": jnp.pad(mla_q_norm_g, (0, pad_qk))[None, :] * (MLA_QK_DIM ** -0.5 * LOG2_E),
        "g_km": jnp.pad(mla_k_norm_g, (0, pad_qk))[None, :],
        "g_qm_rot": partner_lanes(jnp.pad(mla_q_norm_g, (0, pad_qk))[None, :] * (MLA_QK_DIM ** -0.5 * LOG2_E)),
        "g_km_rot": partner_lanes(jnp.pad(mla_k_norm_g, (0, pad_qk))[None, :]),
        "w_oa": w_oa.astype(BF16), "w_ob": w_out[SWA_WIDTH:].astype(BF16),
        "g_ffn": ffn_norm_g[None, :], "rw_hi": rw_hi, "rw_lo": rw_lo, "rb": rb,
    }


def _rope_tables(n_rows):
    half = MLA_ROPE_DIM // 2
    inv = (1.0 / (ROPE_THETA ** (np.arange(half, dtype=np.float32) / half))).astype(np.float32)
    ang = np.arange(n_rows, dtype=np.float32)[:, None] * inv[None, :]
    cos, sin = jnp.asarray(np.cos(ang)), jnp.asarray(np.sin(ang))
    one = jnp.ones((n_rows, MLA_NOPE_DIM), F32)
    zero = jnp.zeros((n_rows, MLA_NOPE_DIM), F32)
    pad1 = jnp.ones((n_rows, LANES - MLA_QK_DIM), F32)
    pad0 = jnp.zeros((n_rows, LANES - MLA_QK_DIM), F32)
    return (jnp.concatenate([one, cos, cos, pad1], axis=1),
            jnp.concatenate([zero, -sin, sin, pad0], axis=1))


def kernel(x, meta_tokens, attn_norm_g, w_in, swa_q_norm_g, swa_k_norm_g, swa_sink, mla_q_lat_norm_g, w_mla_q_up, mla_kv_lat_norm_g, w_mla_kv_up, mla_q_norm_g, mla_k_norm_g, w_out, ffn_norm_g, router_w, router_b, w_gate, b_gate, w_up, b_up, w_down, b_down):
    bsz, n, d = x.shape
    assert bsz == 1 and d == D_MODEL and n % TK_MLA == 0 and attn_norm_g.shape[0] == 1
    wts = _prepare_weights(attn_norm_g[0], w_in[0], swa_q_norm_g[0], swa_k_norm_g[0], mla_q_lat_norm_g[0],
                           w_mla_q_up[0], mla_kv_lat_norm_g[0], w_mla_kv_up[0], mla_q_norm_g[0],
                           mla_k_norm_g[0], w_out[0], ffn_norm_g[0], router_w[0], router_b[0])
    xr = x[0]
    cos_t, sin_t = _rope_tables(N_META + n)

    xm = jnp.pad(meta_tokens.astype(F32), ((0, LANES - N_META), (0, 0)))
    valid_m = (jnp.arange(LANES) < N_META).astype(F32)[None, :]
    _, ks_m, vs_m, _, km_m, vt_m = _proj_call(xm, valid_m, cos_t[:LANES], sin_t[:LANES], wts, LANES)
    qs, ks, vs, qm, km, vt = _proj_call(xr, jnp.ones((1, n), F32), cos_t[N_META:], sin_t[N_META:], wts, TM_PROJ)

    ya = _swa_call(swa_sink[0], qs, ks, vs, ks_m, vs_m)
    yt = _mla_call(qm, km, vt, km_m, vt_m, TQ_MLA, TK_MLA)
    h1, m, ids, rank, gate, cnt = _post_call(xr, ya, yt, wts, TM_POST)

    te = TE_FFN
    counts = cnt[0, :N_EXPERTS].astype(jnp.int32)
    padded = (counts + te - 1) // te * te
    pad_end = jnp.cumsum(padded)
    pad_start = pad_end - padded
    expert = jnp.arange(N_EXPERTS, dtype=jnp.int32)[:, None, None]
    start_of = jnp.sum(jnp.where(ids[None, :TOP_K] == expert, pad_start[:, None, None], 0), axis=0)
    dest = (start_of + rank[:TOP_K]).reshape(-1)
    n_blocks = (n * TOP_K + N_EXPERTS * (te - 1)) // te
    block_start = jnp.arange(n_blocks, dtype=jnp.int32) * te
    block_e = jnp.minimum(jnp.sum((pad_end[None, :] <= block_start[:, None]).astype(jnp.int32), axis=1),
                          N_EXPERTS - 1)
    n_used = (pad_end[-1:] // te).astype(jnp.int32)
    tail_start = jnp.where(padded > 0, pad_end - te, -1).astype(jnp.int32)
    eid = jnp.arange(N_EXPERTS, dtype=jnp.int32)
    later_used = (padded[None, :] > 0) & (eid[None, :] > eid[:, None])
    next_e = jnp.min(jnp.where(later_used, eid[None, :], N_EXPERTS), axis=1)
    next_e = jnp.where(next_e < N_EXPERTS, next_e, -1).astype(jnp.int32)

    xs = _dispatch_call(dest, tail_start, n_used, m, n_blocks * te, TD_DISPATCH, te)
    ys = _ffn_call(block_e, n_used, next_e, xs, w_gate[0], b_gate[0], w_up[0], b_up[0], w_down[0], b_down[0], te)
    out = _combine_call(dest, h1, gate, ys, TC_COMBINE)
    return out[None]
```

```python
import functools

import numpy as np
import jax
import jax.numpy as jnp
from jax import lax
from jax.experimental import pallas as pl
from jax.experimental.pallas import tpu as pltpu

F32 = jnp.float32
BF16 = jnp.bfloat16

D_MODEL = 1024
N_META = 16
BLOCK = 128
WINDOW = 128
NORM_EPS = 1e-6

SWA_HEADS = 8
SWA_KV_HEADS = 2
SWA_GROUP = SWA_HEADS // SWA_KV_HEADS
SWA_HEAD_DIM = 64
SWA_WIDTH = SWA_HEADS * SWA_HEAD_DIM
SWA_KV_WIDTH = SWA_KV_HEADS * SWA_HEAD_DIM

MLA_HEADS = 8
MLA_Q_LORA = 256
MLA_KV_LORA = 128
MLA_NOPE_DIM = 64
MLA_ROPE_DIM = 32
MLA_V_DIM = 64
MLA_QK_DIM = MLA_NOPE_DIM + MLA_ROPE_DIM
MLA_WIDTH = MLA_HEADS * MLA_V_DIM
ROPE_THETA = 10000.0

N_EXPERTS = 32
TOP_K = 4
D_FF = 1024
SWIGLU_LIMIT = 7.0
SWIGLU_ALPHA = 1.702

LANES = 128
VT_ROWS = 80
NEG_BIG = -1e30
LOG2_E = 1.4426950408889634
VMEM_LIMIT = 48 * 1024 * 1024
FFN_VMEM_LIMIT = 56 * 1024 * 1024

TM_PROJ = 512
TQ_MLA = 512
TK_MLA = 512
MLA_UNROLL = 16
ROW_DMA_UNROLL = 8
TM_POST = 512
TD_DISPATCH = 256
TE_FFN = 512
TC_COMBINE = 256

_NT = (((1,), (1,)), ((), ()))


def _rms(v, n):
    return v * lax.rsqrt(jnp.sum(v * v, axis=-1, keepdims=True) * (1.0 / n) + NORM_EPS)


TILE_ROWS = D_MODEL // LANES


def _store_token_tiles(ref, v):
    rows = v.shape[0]
    for s in range(TILE_ROWS):
        ref[pl.ds(s, rows, stride=TILE_ROWS), :] = v[:, s * LANES:(s + 1) * LANES]


def _load_token_tiles(ref, rows):
    return jnp.concatenate([ref[pl.ds(s, rows, stride=TILE_ROWS), :] for s in range(TILE_ROWS)], axis=1)


def _proj_kernel(x_ref, valid_ref, cos_ref, sin_ref, g_attn_ref, w_in_ref, g_qs_ref, g_ks_ref,
                 g_qlat_ref, w_qup_ref, g_kvlat_ref, w_k_ref, w_vt_ref, ones_ref, g_qm_ref, g_km_ref,
                 g_qm_rot_ref, g_km_rot_ref,
                 qs_ref, ks_ref, vs_ref, qm_ref, km_ref, vt_ref):
    tm = x_ref.shape[0]
    a = _rms(x_ref[...], D_MODEL) * g_attn_ref[...]
    proj = jnp.dot(a.astype(BF16), w_in_ref[...], preferred_element_type=F32)

    lane = lax.broadcasted_iota(jnp.int32, (tm, LANES), 1)
    lo = lane < SWA_HEAD_DIM

    def seg_norm(v, g):
        v2 = v * v
        s_all = jnp.sum(v2, axis=-1, keepdims=True)
        s_lo = jnp.sum(jnp.where(lo, v2, 0.0), axis=-1, keepdims=True)
        ms = jnp.where(lo, s_lo, s_all - s_lo) * (1.0 / SWA_HEAD_DIM)
        return v * lax.rsqrt(ms + NORM_EPS) * g

    g_qs = g_qs_ref[...]
    for j in range(SWA_WIDTH // LANES):
        sl = slice(j * LANES, (j + 1) * LANES)
        qs_ref[:, sl] = seg_norm(proj[:, sl], g_qs).astype(BF16)
    o_k = SWA_WIDTH
    o_v = o_k + SWA_KV_WIDTH
    o_ql = o_v + SWA_KV_WIDTH
    o_kv = o_ql + MLA_Q_LORA
    o_kr = o_kv + MLA_KV_LORA
    ks_ref[...] = seg_norm(proj[:, o_k:o_v], g_ks_ref[...]).astype(BF16)
    vs_ref[...] = proj[:, o_v:o_ql].astype(BF16)

    cosv = cos_ref[...]
    sinv = sin_ref[...]

    def norm_rope(v, v_rot, g, g_rot):
        ms = jnp.sum(v * v, axis=-1, keepdims=True) * (1.0 / MLA_QK_DIM)
        return (v * g * cosv + v_rot * g_rot * sinv) * lax.rsqrt(ms + NORM_EPS)

    hw = MLA_HEADS * LANES
    qln = _rms(proj[:, o_ql:o_kv], MLA_Q_LORA) * g_qlat_ref[...]
    qup = jnp.dot(qln.astype(BF16), w_qup_ref[...], preferred_element_type=F32)
    g_qm = g_qm_ref[...]
    g_qm_rot = g_qm_rot_ref[...]
    for h in range(MLA_HEADS):
        sl = slice(h * LANES, (h + 1) * LANES)
        sl_rot = slice(hw + h * LANES, hw + (h + 1) * LANES)
        qm_ref[h] = norm_rope(qup[:, sl], qup[:, sl_rot], g_qm, g_qm_rot).T.astype(BF16)

    kvn = (_rms(proj[:, o_kv:o_kr], MLA_KV_LORA) * g_kvlat_ref[...]).astype(BF16)
    knope = jnp.dot(kvn, w_k_ref[...], preferred_element_type=F32)
    krope = proj[:, o_kr:o_kr + LANES]
    krope_rot = proj[:, o_kr + LANES:o_kr + 2 * LANES]
    g_km = g_km_ref[...]
    g_km_rot = g_km_rot_ref[...]
    for h in range(MLA_HEADS):
        km_ref[h] = norm_rope(knope[:, h * LANES:(h + 1) * LANES] + krope, krope_rot, g_km, g_km_rot).astype(BF16)

    vt = lax.dot_general(w_vt_ref[...], kvn, _NT, preferred_element_type=F32)
    vt = vt + ones_ref[...] * valid_ref[...]
    for h in range(MLA_HEADS):
        vt_ref[h] = vt[h * VT_ROWS:(h + 1) * VT_ROWS].astype(BF16)


def _proj_call(x, valid, cos_t, sin_t, wts, tm):
    n = x.shape[0]
    w_names = ("g_attn", "w_in", "g_qs", "g_ks", "g_qlat", "w_qup", "g_kvlat", "w_k", "w_vt", "ones", "g_qm", "g_km",
               "g_qm_rot", "g_km_rot")
    w_args = [wts[k] for k in w_names]

    def full(a):
        return pl.BlockSpec(a.shape, lambda i: (0,) * a.ndim)

    in_specs = [pl.BlockSpec((tm, D_MODEL), lambda i: (i, 0)),
                pl.BlockSpec((1, tm), lambda i: (0, i)),
                pl.BlockSpec((tm, LANES), lambda i: (i, 0)),
                pl.BlockSpec((tm, LANES), lambda i: (i, 0))] + [full(a) for a in w_args]
    out_shape = (jax.ShapeDtypeStruct((n, SWA_WIDTH), BF16),
                 jax.ShapeDtypeStruct((n, SWA_KV_WIDTH), BF16),
                 jax.ShapeDtypeStruct((n, SWA_KV_WIDTH), BF16),
                 jax.ShapeDtypeStruct((MLA_HEADS, LANES, n), BF16),
                 jax.ShapeDtypeStruct((MLA_HEADS, n, LANES), BF16),
                 jax.ShapeDtypeStruct((MLA_HEADS, VT_ROWS, n), BF16))
    out_specs = (pl.BlockSpec((tm, SWA_WIDTH), lambda i: (i, 0)),
                 pl.BlockSpec((tm, SWA_KV_WIDTH), lambda i: (i, 0)),
                 pl.BlockSpec((tm, SWA_KV_WIDTH), lambda i: (i, 0)),
                 pl.BlockSpec((MLA_HEADS, LANES, tm), lambda i: (0, 0, i)),
                 pl.BlockSpec((MLA_HEADS, tm, LANES), lambda i: (0, i, 0)),
                 pl.BlockSpec((MLA_HEADS, VT_ROWS, tm), lambda i: (0, 0, i)))
    return pl.pallas_call(
        _proj_kernel, grid=(n // tm,), in_specs=in_specs, out_specs=out_specs, out_shape=out_shape,
        compiler_params=pltpu.CompilerParams(dimension_semantics=("arbitrary",), vmem_limit_bytes=VMEM_LIMIT),
        name="proj",
    )(x, valid, cos_t, sin_t, *w_args)


def _swa_bias_table():
    nkeys = 3 * BLOCK + LANES
    r = np.arange(BLOCK)[:, None]
    c = np.arange(nkeys)[None, :]
    d = np.abs(BLOCK + r - c)
    in_win = (d <= WINDOW) & (c < 3 * BLOCK)
    is_meta = (c >= 3 * BLOCK) & (c < 3 * BLOCK + N_META)
    slopes = 2.0 ** (-8.0 * np.arange(1, SWA_HEADS + 1) / SWA_HEADS)
    bias = np.where(in_win[None], -slopes[:, None, None] * d[None], np.where(is_meta[None], 0.0, NEG_BIG))
    return jnp.asarray(bias, F32)


def _swa_kernel(sink_ref, q_ref, kp_ref, kc_ref, kn_ref, vp_ref, vc_ref, vn_ref, km_ref, vm_ref, bias_ref, o_ref, s_scr):
    b = pl.program_id(0)
    nb = pl.num_programs(0)
    nkeys = 3 * BLOCK + LANES
    kall = jnp.concatenate([kp_ref[...], kc_ref[...], kn_ref[...], km_ref[...]], axis=0)
    vall = jnp.concatenate([vp_ref[...], vc_ref[...], vn_ref[...], vm_ref[...]], axis=0)

    c = lax.broadcasted_iota(jnp.int32, (1, nkeys), 1)
    lo_edge = jnp.where(b > 0, 0, BLOCK)
    hi_edge = jnp.where(b < nb - 1, 3 * BLOCK, 2 * BLOCK)
    edge = jnp.where((c < lo_edge) | ((c >= hi_edge) & (c < 3 * BLOCK)), NEG_BIG, 0.0)

    lane = lax.broadcasted_iota(jnp.int32, (BLOCK, LANES), 1)
    lo = lane < SWA_HEAD_DIM
    for j in range(SWA_GROUP):
        qg = q_ref[:, j * LANES:(j + 1) * LANES].astype(F32)
        q2 = jnp.concatenate([jnp.where(lo, qg, 0.0), jnp.where(lo, 0.0, qg)], axis=0).astype(BF16)
        s_scr[j] = lax.dot_general(q2, kall, _NT, preferred_element_type=F32)
    for j in range(SWA_GROUP):
        outs = []
        for half, h in ((0, j), (1, j + SWA_GROUP)):
            sh = s_scr[j, half * BLOCK:(half + 1) * BLOCK] + bias_ref[h] + edge
            sink = sink_ref[h]
            m = jnp.maximum(jnp.max(sh, axis=-1, keepdims=True), sink)
            p = jnp.exp(sh - m)
            den = jnp.sum(p, axis=-1, keepdims=True) + jnp.exp(sink - m)
            o = jnp.dot(p.astype(BF16), vall, preferred_element_type=F32)
            outs.append(o / den)
        o_ref[:, j * LANES:(j + 1) * LANES] = jnp.where(lo, outs[0], outs[1]).astype(BF16)


def _swa_call(sink, qs, ks, vs, ks_meta, vs_meta):
    n = qs.shape[0]
    nb = n // BLOCK
    bias = _swa_bias_table()
    kv_prev = pl.BlockSpec((BLOCK, SWA_KV_WIDTH), lambda b: (jnp.maximum(b - 1, 0), 0))
    kv_cur = pl.BlockSpec((BLOCK, SWA_KV_WIDTH), lambda b: (b, 0))
    kv_next = pl.BlockSpec((BLOCK, SWA_KV_WIDTH), lambda b: (jnp.minimum(b + 1, nb - 1), 0))
    meta = pl.BlockSpec((LANES, SWA_KV_WIDTH), lambda b: (0, 0))
    return pl.pallas_call(
        _swa_kernel,
        grid=(nb,),
        in_specs=[pl.BlockSpec(memory_space=pltpu.SMEM),
                  pl.BlockSpec((BLOCK, SWA_WIDTH), lambda b: (b, 0)),
                  kv_prev, kv_cur, kv_next, kv_prev, kv_cur, kv_next, meta, meta,
                  pl.BlockSpec(bias.shape, lambda b: (0, 0, 0))],
        out_specs=pl.BlockSpec((BLOCK, SWA_WIDTH), lambda b: (b, 0)),
        out_shape=jax.ShapeDtypeStruct((n, SWA_WIDTH), BF16),
        scratch_shapes=[pltpu.VMEM((SWA_GROUP, 2 * BLOCK, 3 * BLOCK + LANES), F32)],
        compiler_params=pltpu.CompilerParams(dimension_semantics=("arbitrary",), vmem_limit_bytes=VMEM_LIMIT),
        name="swa",
    )(sink, qs, ks, ks, ks, vs, vs, vs, ks_meta, vs_meta, bias)


def _mla_kernel(q_ref, k_ref, vt_ref, km_ref, vtm_ref, o_ref, s_scr, *, tk, nk, unroll):
    qt = q_ref[0]
    tq = qt.shape[1]

    def scores(off):
        return jnp.dot(k_ref[0, pl.ds(off, tk), :], qt, preferred_element_type=F32)

    def soft_pv(s, vt, m, acc):
        m_new = jnp.maximum(m, jnp.max(s, axis=0, keepdims=True))
        alpha = jnp.exp2(m - m_new)
        p = jnp.exp2(s - m_new).astype(BF16)
        acc = alpha * acc + jnp.dot(vt, p, preferred_element_type=F32)
        return m_new, acc

    m0 = jnp.full((1, tq), NEG_BIG, F32)
    acc0 = jnp.zeros((VT_ROWS, tq), F32)
    s_meta = jnp.dot(km_ref[0], qt, preferred_element_type=F32)
    s_scr[0] = scores(0)
    m, acc = soft_pv(s_meta, vtm_ref[0], m0, acc0)

    def body(i, carry):
        m, acc = carry
        for u in range(unroll):
            t = i * unroll + u
            cur = u % 2
            s_next = scores(pl.multiple_of(jnp.minimum(t + 1, nk - 1) * tk, tk))
            off = pl.multiple_of(t * tk, tk)
            m, acc = soft_pv(s_scr[cur], vt_ref[0, :, pl.ds(off, tk)], m, acc)
            s_scr[1 - cur] = s_next
        return m, acc

    m, acc = lax.fori_loop(0, nk // unroll, body, (m, acc))
    o_ref[...] = (acc[0:MLA_V_DIM] / acc[MLA_V_DIM:MLA_V_DIM + 1]).astype(BF16)


def _mla_call(qm, km, vt, km_meta, vt_meta, tq, tk):
    n = qm.shape[2]
    nk = n // tk
    unroll = MLA_UNROLL if nk % MLA_UNROLL == 0 else 2
    assert nk % unroll == 0
    return pl.pallas_call(
        functools.partial(_mla_kernel, tk=tk, nk=nk, unroll=unroll),
        grid=(MLA_HEADS, n // tq),
        in_specs=[pl.BlockSpec((1, LANES, tq), lambda h, i: (h, 0, i)),
                  pl.BlockSpec((1, n, LANES), lambda h, i: (h, 0, 0)),
                  pl.BlockSpec((1, VT_ROWS, n), lambda h, i: (h, 0, 0)),
                  pl.BlockSpec((1, LANES, LANES), lambda h, i: (h, 0, 0)),
                  pl.BlockSpec((1, VT_ROWS, LANES), lambda h, i: (h, 0, 0))],
        out_specs=pl.BlockSpec((MLA_V_DIM, tq), lambda h, i: (h, i)),
        out_shape=jax.ShapeDtypeStruct((MLA_WIDTH, n), BF16),
        scratch_shapes=[pltpu.VMEM((2, tk, tq), F32)],
        compiler_params=pltpu.CompilerParams(dimension_semantics=("arbitrary", "arbitrary"),
                                             vmem_limit_bytes=VMEM_LIMIT),
        name="mla",
    )(qm, km, vt, km_meta, vt_meta)


def _post_kernel(x_ref, ya_ref, yt_ref, w_oa_ref, w_ob_ref, g_ffn_ref, rw_hi_ref, rw_lo_ref, rb_ref,
                 h1_ref, m_ref, ids_ref, rank_ref, gate_ref, cnt_ref, base_ref):
    tm = x_ref.shape[0]

    @pl.when(pl.program_id(0) == 0)
    def _():
        base_ref[...] = jnp.zeros_like(base_ref)

    yb = yt_ref[...].T
    h1 = (x_ref[...]
          + jnp.dot(ya_ref[...], w_oa_ref[...], preferred_element_type=F32)
          + jnp.dot(yb, w_ob_ref[...], preferred_element_type=F32))
    h1_ref[...] = h1
    m = _rms(h1, D_MODEL) * g_ffn_ref[...]
    _store_token_tiles(m_ref, m)

    m_hi = m.astype(BF16)
    m_lo = (m - m_hi.astype(F32)).astype(BF16)
    rw_hi = rw_hi_ref[...]
    logits = (jnp.dot(m_hi, rw_hi, preferred_element_type=F32)
              + jnp.dot(m_lo, rw_hi, preferred_element_type=F32)
              + jnp.dot(m_hi, rw_lo_ref[...], preferred_element_type=F32)
              + rb_ref[...])

    lane = lax.broadcasted_iota(jnp.int32, (tm, LANES), 1)
    lane_f = lane.astype(F32)
    work = logits
    vals, idxs = [], []
    for _k in range(TOP_K):
        mx = jnp.max(work, axis=-1, keepdims=True)
        ix = jnp.min(jnp.where(work == mx, lane_f, float(LANES)), axis=-1, keepdims=True)
        vals.append(mx)
        idxs.append(ix)
        work = jnp.where(lane_f == ix, -3e38, work)
    exps = [jnp.exp(v - vals[0]) for v in vals]
    den = exps[0] + exps[1] + exps[2] + exps[3]

    hits = [lane_f == ix for ix in idxs]
    onehot = jnp.zeros((tm, LANES), F32)
    for hk in hits:
        onehot = onehot + jnp.where(hk, 1.0, 0.0)
    row = lax.broadcasted_iota(jnp.int32, (tm, tm), 0)
    col = lax.broadcasted_iota(jnp.int32, (tm, tm), 1)
    tri = jnp.where(row > col, 1.0, 0.0).astype(BF16)
    before = jnp.dot(tri, onehot.astype(BF16), preferred_element_type=F32) + base_ref[...]
    ids = jnp.zeros((tm, LANES), F32)
    rank = jnp.zeros((tm, LANES), F32)
    gate = jnp.zeros((tm, LANES), F32)
    for k in range(TOP_K):
        rk = jnp.sum(jnp.where(hits[k], before, 0.0), axis=-1, keepdims=True)
        sel = lane == k
        ids = jnp.where(sel, idxs[k], ids)
        rank = jnp.where(sel, rk, rank)
        gate = jnp.where(sel, exps[k] / den, gate)
    ids_ref[...] = ids.T[0:8].astype(jnp.int32)
    rank_ref[...] = rank.T[0:8].astype(jnp.int32)
    gate_ref[...] = gate
    base = base_ref[...] + jnp.sum(onehot, axis=0, keepdims=True)
    base_ref[...] = base
    cnt_ref[...] = base


def _post_call(x, ya, yt, wts, tm):
    n = x.shape[0]
    w_names = ("w_oa", "w_ob", "g_ffn", "rw_hi", "rw_lo", "rb")
    w_args = [wts[k] for k in w_names]

    def full(a):
        return pl.BlockSpec(a.shape, lambda i: (0,) * a.ndim)

    row = lambda w: pl.BlockSpec((tm, w), lambda i: (i, 0))
    choice = pl.BlockSpec((8, tm), lambda i: (0, i))
    return pl.pallas_call(
        _post_kernel, grid=(n // tm,),
        in_specs=[row(D_MODEL), row(SWA_WIDTH), pl.BlockSpec((MLA_WIDTH, tm), lambda i: (0, i))]
        + [full(a) for a in w_args],
        out_specs=(row(D_MODEL), pl.BlockSpec((tm * TILE_ROWS, LANES), lambda i: (i, 0)), choice, choice, row(LANES),
                   pl.BlockSpec((1, LANES), lambda i: (0, 0))),
        out_shape=(jax.ShapeDtypeStruct((n, D_MODEL), F32), jax.ShapeDtypeStruct((n * TILE_ROWS, LANES), F32),
                   jax.ShapeDtypeStruct((8, n), jnp.int32), jax.ShapeDtypeStruct((8, n), jnp.int32),
                   jax.ShapeDtypeStruct((n, LANES), F32), jax.ShapeDtypeStruct((1, LANES), F32)),
        scratch_shapes=[pltpu.VMEM((1, LANES), F32)],
        compiler_params=pltpu.CompilerParams(dimension_semantics=("arbitrary",), vmem_limit_bytes=VMEM_LIMIT),
        name="post",
    )(x, ya, yt, *w_args)


def _token_copy(src, src_tok, dst, dst_tok, sem):
    rows = lambda tok: pl.ds(pl.multiple_of(tok * TILE_ROWS, TILE_ROWS), TILE_ROWS)
    return pltpu.make_async_copy(src.at[rows(src_tok), :], dst.at[rows(dst_tok), :], sem)


def _dispatch_kernel(dest_ref, tail_ref, nused_ref, m_ref, m_hbm, xs_ref, zbuf, sem, zsem, hsem, *, td, n, te):
    block_rows = te * TILE_ROWS

    def zero_copy(slot):
        start = pl.multiple_of(slot * TILE_ROWS, block_rows)
        return pltpu.make_async_copy(zbuf, xs_ref.at[pl.ds(start, block_rows), :], zsem)

    @pl.when(pl.program_id(0) == 0)
    def _():
        zbuf[...] = jnp.zeros_like(zbuf)
        n_blocks = xs_ref.shape[0] // block_rows
        for e in range(N_EXPERTS):
            @pl.when(tail_ref[e] >= 0)
            def _():
                zero_copy(jnp.maximum(tail_ref[e], 0)).start()

        def start_unused(b, carry):
            zero_copy(b * te).start()
            return carry

        lax.fori_loop(nused_ref[0], n_blocks, start_unused, 0)
        for e in range(N_EXPERTS):
            @pl.when(tail_ref[e] >= 0)
            def _():
                zero_copy(0).wait()

        def wait_unused(b, carry):
            zero_copy(0).wait()
            return carry

        lax.fori_loop(nused_ref[0], n_blocks, wait_unused, 0)

    base = pl.program_id(0) * td

    half_k = TOP_K // 2

    def issue(tb, carry):
        for u in range(ROW_DMA_UNROLL):
            t = tb * ROW_DMA_UNROLL + u
            for k in range(half_k):
                _token_copy(m_ref, t, xs_ref, dest_ref[k * n + base + t], sem).start(priority=k % 2)
            for k in range(half_k, TOP_K):
                _token_copy(m_hbm, base + t, xs_ref, dest_ref[k * n + base + t], hsem).start(priority=k % 2)
        return carry

    lax.fori_loop(0, td // ROW_DMA_UNROLL, issue, 0)

    def drain(tb, carry):
        for _u in range(ROW_DMA_UNROLL * half_k):
            _token_copy(m_ref, 0, xs_ref, 0, sem).wait()
        for _u in range(ROW_DMA_UNROLL * (TOP_K - half_k)):
            _token_copy(m_hbm, 0, xs_ref, 0, hsem).wait()
        return carry

    lax.fori_loop(0, td // ROW_DMA_UNROLL, drain, 0)


def _dispatch_call(dest_flat, tail_start, n_used, m, n_slots, td, te):
    n = m.shape[0] // TILE_ROWS
    grid_spec = pltpu.PrefetchScalarGridSpec(
        num_scalar_prefetch=3, grid=(n // td,),
        in_specs=[pl.BlockSpec((td * TILE_ROWS, LANES), lambda i, d, z, u: (i, 0)),
                  pl.BlockSpec(memory_space=pl.ANY)],
        out_specs=pl.BlockSpec(memory_space=pl.ANY),
        scratch_shapes=[pltpu.VMEM((te * TILE_ROWS, LANES), F32), pltpu.SemaphoreType.DMA,
                        pltpu.SemaphoreType.DMA, pltpu.SemaphoreType.DMA])
    return pl.pallas_call(
        functools.partial(_dispatch_kernel, td=td, n=n, te=te), grid_spec=grid_spec,
        out_shape=jax.ShapeDtypeStruct((n_slots * TILE_ROWS, LANES), F32),
        compiler_params=pltpu.CompilerParams(dimension_semantics=("arbitrary",), vmem_limit_bytes=VMEM_LIMIT),
        name="dispatch",
    )(dest_flat, tail_start, n_used, m, m)


def _ffn_kernel(be_ref, nused_ref, next_ref, xs_ref, bg_ref, bu_ref, bd_ref, wg_hbm, wu_hbm, wd_hbm, ys_ref,
                wbuf, wg_s, wu_s, wd_s, slot_ref, wsem, *, te):
    b = pl.program_id(0)
    used = b < nused_ref[0]
    e = be_ref[b]
    fresh = (b == 0) | (e != be_ref[jnp.maximum(b - 1, 0)])

    def weight_copies(expert, slot):
        return [pltpu.make_async_copy(w.at[expert], wbuf.at[slot, j], wsem.at[slot])
                for j, w in enumerate((wg_hbm, wu_hbm, wd_hbm))]

    @pl.when(b == 0)
    def _():
        slot_ref[0] = 0
        for c in weight_copies(e, 0):
            c.start()

    @pl.when(fresh & used)
    def _():
        slot = slot_ref[0]
        for c in weight_copies(e, slot):
            c.wait()
        nxt = next_ref[e]

        @pl.when(nxt >= 0)
        def _():
            for c in weight_copies(jnp.maximum(nxt, 0), 1 - slot):
                c.start()

        wg_s[...] = wbuf[slot, 0].astype(BF16)
        wu_s[...] = wbuf[slot, 1].astype(BF16)
        wd_s[...] = wbuf[slot, 2].astype(BF16)
        slot_ref[0] = 1 - slot

    @pl.when(used)
    def _():
        x = _load_token_tiles(xs_ref, te).astype(BF16)
        g = jnp.dot(x, wg_s[...], preferred_element_type=F32) + bg_ref[0]
        u = jnp.dot(x, wu_s[...], preferred_element_type=F32) + bu_ref[0]
        g = jnp.minimum(g, SWIGLU_LIMIT)
        u = jnp.clip(u, -SWIGLU_LIMIT, SWIGLU_LIMIT)
        act = (u + 1.0) * (g * jax.nn.sigmoid(SWIGLU_ALPHA * g))
        y = jnp.dot(act.astype(BF16), wd_s[...], preferred_element_type=F32) + bd_ref[0]
        _store_token_tiles(ys_ref, y)

    @pl.when(jnp.logical_not(used))
    def _():
        ys_ref[...] = jnp.zeros_like(ys_ref)


def _ffn_call(block_e, n_used, next_e, xs, w_gate, b_gate, w_up, b_up, w_down, b_down, te):
    n_slots = xs.shape[0] // TILE_ROWS
    bspec = lambda c: pl.BlockSpec((1, 1, c), lambda b, be, nu, nx: (be[b], 0, 0))
    hbm = pl.BlockSpec(memory_space=pl.ANY)
    grid_spec = pltpu.PrefetchScalarGridSpec(
        num_scalar_prefetch=3, grid=(n_slots // te,),
        in_specs=[pl.BlockSpec((te * TILE_ROWS, LANES), lambda b, be, nu, nx: (jnp.minimum(b, nu[0] - 1), 0)),
                  bspec(D_FF), bspec(D_FF), bspec(D_MODEL), hbm, hbm, hbm],
        out_specs=pl.BlockSpec((te * TILE_ROWS, LANES), lambda b, be, nu, nx: (b, 0)),
        scratch_shapes=[pltpu.VMEM((2, 3, D_MODEL, D_FF), F32),
                        pltpu.VMEM((D_MODEL, D_FF), BF16), pltpu.VMEM((D_MODEL, D_FF), BF16),
                        pltpu.VMEM((D_FF, D_MODEL), BF16), pltpu.SMEM((1,), jnp.int32),
                        pltpu.SemaphoreType.DMA((2,))])
    return pl.pallas_call(
        functools.partial(_ffn_kernel, te=te), grid_spec=grid_spec,
        out_shape=jax.ShapeDtypeStruct((n_slots * TILE_ROWS, LANES), F32),
        compiler_params=pltpu.CompilerParams(dimension_semantics=("arbitrary",), vmem_limit_bytes=FFN_VMEM_LIMIT),
        name="ffn",
    )(block_e, n_used, next_e, xs, b_gate[:, None, :], b_up[:, None, :], b_down[:, None, :], w_gate, w_up, w_down)


def _combine_kernel(dest_ref, h1_ref, gate_ref, ys_ref, o_ref, buf, sem, *, tc, n):
    i = pl.program_id(0)

    def gather(step, slot):
        base = step * tc

        def issue(tb, carry):
            for u in range(ROW_DMA_UNROLL):
                t = tb * ROW_DMA_UNROLL + u
                for k in range(TOP_K):
                    _token_copy(ys_ref, dest_ref[k * n + base + t], buf.at[slot, k], t,
                                sem.at[slot]).start(priority=k % 2)
            return carry

        lax.fori_loop(0, tc // ROW_DMA_UNROLL, issue, 0)

    @pl.when(i == 0)
    def _():
        gather(0, 0)

    slot = i % 2

    @pl.when(i + 1 < pl.num_programs(0))
    def _():
        gather(i + 1, 1 - slot)

    def drain(tb, carry):
        for _u in range(ROW_DMA_UNROLL):
            for k in range(TOP_K):
                _token_copy(ys_ref, 0, buf.at[slot, k], 0, sem.at[slot]).wait()
        return carry

    lax.fori_loop(0, tc // ROW_DMA_UNROLL, drain, 0)
    gate = gate_ref[...]
    out = h1_ref[...]
    for k in range(TOP_K):
        out = out + gate[:, k:k + 1] * _load_token_tiles(buf.at[slot, k], tc)
    o_ref[...] = out


def _combine_call(dest_flat, h1, gate, ys, tc):
    n = h1.shape[0]
    grid_spec = pltpu.PrefetchScalarGridSpec(
        num_scalar_prefetch=1, grid=(n // tc,),
        in_specs=[pl.BlockSpec((tc, D_MODEL), lambda i, d: (i, 0)),
                  pl.BlockSpec((tc, LANES), lambda i, d: (i, 0)),
                  pl.BlockSpec(memory_space=pl.ANY)],
        out_specs=pl.BlockSpec((tc, D_MODEL), lambda i, d: (i, 0)),
        scratch_shapes=[pltpu.VMEM((2, TOP_K, tc * TILE_ROWS, LANES), F32), pltpu.SemaphoreType.DMA((2,))])
    return pl.pallas_call(
        functools.partial(_combine_kernel, tc=tc, n=n), grid_spec=grid_spec,
        out_shape=jax.ShapeDtypeStruct((n, D_MODEL), F32),
        compiler_params=pltpu.CompilerParams(dimension_semantics=("arbitrary",), vmem_limit_bytes=VMEM_LIMIT),
        name="combine",
    )(dest_flat, h1, gate, ys)


def _prepare_weights(attn_norm_g, w_in, swa_q_norm_g, swa_k_norm_g, mla_q_lat_norm_g, w_mla_q_up,
                     mla_kv_lat_norm_g, w_mla_kv_up, mla_q_norm_g, mla_k_norm_g, w_out, ffn_norm_g,
                     router_w, router_b):
    o_k = SWA_WIDTH
    o_v = o_k + SWA_KV_WIDTH
    o_ql = o_v + SWA_KV_WIDTH
    o_kv = o_ql + MLA_Q_LORA
    o_kr = o_kv + MLA_KV_LORA
    perm = np.arange(SWA_HEADS).reshape(SWA_KV_HEADS, SWA_GROUP).T.reshape(-1)
    w_qs = w_in[:, :o_k].reshape(D_MODEL, SWA_HEADS, SWA_HEAD_DIM)[:, perm].reshape(D_MODEL, SWA_WIDTH)
    w_kr = jnp.zeros((D_MODEL, LANES), F32).at[:, MLA_NOPE_DIM:MLA_QK_DIM].set(w_in[:, o_kr:])
    half = MLA_ROPE_DIM // 2
    r0, r1, r2 = MLA_NOPE_DIM, MLA_NOPE_DIM + half, MLA_QK_DIM

    def partner_lanes(a):
        z = jnp.zeros_like(a)
        return z.at[..., r0:r1].set(a[..., r1:r2]).at[..., r1:r2].set(a[..., r0:r1])

    w_in_r = jnp.concatenate([w_qs, w_in[:, o_k:o_kr], w_kr, partner_lanes(w_kr)], axis=1).astype(BF16)

    pad_qk = LANES - MLA_QK_DIM
    w_qup = jnp.pad(w_mla_q_up.reshape(MLA_Q_LORA, MLA_HEADS, MLA_QK_DIM), ((0, 0), (0, 0), (0, pad_qk)))
    w_qup = jnp.concatenate([w_qup.reshape(MLA_Q_LORA, MLA_HEADS * LANES),
                             partner_lanes(w_qup).reshape(MLA_Q_LORA, MLA_HEADS * LANES)], axis=1).astype(BF16)
    kv_up = w_mla_kv_up.reshape(MLA_KV_LORA, MLA_HEADS, MLA_NOPE_DIM + MLA_V_DIM)
    w_k = jnp.pad(kv_up[:, :, :MLA_NOPE_DIM], ((0, 0), (0, 0), (0, LANES - MLA_NOPE_DIM)))
    w_k = w_k.reshape(MLA_KV_LORA, MLA_HEADS * LANES).astype(BF16)
    w_vt = jnp.transpose(kv_up[:, :, MLA_NOPE_DIM:], (1, 2, 0))
    w_vt = jnp.pad(w_vt, ((0, 0), (0, VT_ROWS - MLA_V_DIM), (0, 0))).reshape(MLA_HEADS * VT_ROWS, MLA_KV_LORA)
    ones = np.zeros((MLA_HEADS, VT_ROWS, 1), np.float32)
    ones[:, MLA_V_DIM, 0] = 1.0

    w_oa = w_out[:SWA_WIDTH].reshape(SWA_HEADS, SWA_HEAD_DIM, D_MODEL)[perm].reshape(SWA_WIDTH, D_MODEL)
    rw = jnp.pad(router_w, ((0, 0), (0, LANES - N_EXPERTS)))
    rw_hi = rw.astype(BF16)
    rw_lo = (rw - rw_hi.astype(F32)).astype(BF16)
    rb = jnp.concatenate([router_b.astype(F32), jnp.full((LANES - N_EXPERTS,), NEG_BIG, F32)])[None, :]
    return {
        "g_attn": attn_norm_g[None, :], "w_in": w_in_r,
        "g_qs": jnp.tile(swa_q_norm_g, 2)[None, :] * (SWA_HEAD_DIM ** -0.5),
        "g_ks": jnp.tile(swa_k_norm_g, 2)[None, :],
        "g_qlat": mla_q_lat_norm_g[None, :], "w_qup": w_qup,
        "g_kvlat": mla_kv_lat_norm_g[None, :], "w_k": w_k, "w_vt": w_vt.astype(BF16),
        "ones": jnp.asarray(ones.reshape(MLA_HEADS * VT_ROWS, 1)),
        "g_qm": jnp.pad(mla_q_norm_g, (0, pad_qk))[None, :] * (MLA_QK_DIM ** -0.5 * LOG2_E),
        "g_km": jnp.pad(mla_k_norm_g, (0, pad_qk))[None, :],
        "g_qm_rot": partner_lanes(jnp.pad(mla_q_norm_g, (0, pad_qk))[None, :] * (MLA_QK_DIM ** -0.5 * LOG2_E)),
        "g_km_rot": partner_lanes(jnp.pad(mla_k_norm_g, (0, pad_qk))[None, :]),
        "w_oa": w_oa.astype(BF16), "w_ob": w_out[SWA_WIDTH:].astype(BF16),
        "g_ffn": ffn_norm_g[None, :], "rw_hi": rw_hi, "rw_lo": rw_lo, "rb": rb,
    }


def _rope_tables(n_rows):
    half = MLA_ROPE_DIM // 2
    inv = (1.0 / (ROPE_THETA ** (np.arange(half, dtype=np.float32) / half))).astype(np.float32)
    ang = np.arange(n_rows, dtype=np.float32)[:, None] * inv[None, :]
    cos, sin = jnp.asarray(np.cos(ang)), jnp.asarray(np.sin(ang))
    one = jnp.ones((n_rows, MLA_NOPE_DIM), F32)
    zero = jnp.zeros((n_rows, MLA_NOPE_DIM), F32)
    pad1 = jnp.ones((n_rows, LANES - MLA_QK_DIM), F32)
    pad0 = jnp.zeros((n_rows, LANES - MLA_QK_DIM), F32)
    return (jnp.concatenate([one, cos, cos, pad1], axis=1),
            jnp.concatenate([zero, -sin, sin, pad0], axis=1))


def kernel(x, meta_tokens, attn_norm_g, w_in, swa_q_norm_g, swa_k_norm_g, swa_sink, mla_q_lat_norm_g, w_mla_q_up, mla_kv_lat_norm_g, w_mla_kv_up, mla_q_norm_g, mla_k_norm_g, w_out, ffn_norm_g, router_w, router_b, w_gate, b_gate, w_up, b_up, w_down, b_down):
    bsz, n, d = x.shape
    assert bsz == 1 and d == D_MODEL and n % TK_MLA == 0 and attn_norm_g.shape[0] == 1
    wts = _prepare_weights(attn_norm_g[0], w_in[0], swa_q_norm_g[0], swa_k_norm_g[0], mla_q_lat_norm_g[0],
                           w_mla_q_up[0], mla_kv_lat_norm_g[0], w_mla_kv_up[0], mla_q_norm_g[0],
                           mla_k_norm_g[0], w_out[0], ffn_norm_g[0], router_w[0], router_b[0])
    xr = x[0]
    cos_t, sin_t = _rope_tables(N_META + n)

    xm = jnp.pad(meta_tokens.astype(F32), ((0, LANES - N_META), (0, 0)))
    valid_m = (jnp.arange(LANES) < N_META).astype(F32)[None, :]
    _, ks_m, vs_m, _, km_m, vt_m = _proj_call(xm, valid_m, cos_t[:LANES], sin_t[:LANES], wts, LANES)
    qs, ks, vs, qm, km, vt = _proj_call(xr, jnp.ones((1, n), F32), cos_t[N_META:], sin_t[N_META:], wts, TM_PROJ)

    ya = _swa_call(swa_sink[0], qs, ks, vs, ks_m, vs_m)
    yt = _mla_call(qm, km, vt, km_m, vt_m, TQ_MLA, TK_MLA)
    h1, m, ids, rank, gate, cnt = _post_call(xr, ya, yt, wts, TM_POST)

    te = TE_FFN
    counts = cnt[0, :N_EXPERTS].astype(jnp.int32)
    padded = (counts + te - 1) // te * te
    pad_end = jnp.cumsum(padded)
    pad_start = pad_end - padded
    expert = jnp.arange(N_EXPERTS, dtype=jnp.int32)[:, None, None]
    start_of = jnp.sum(jnp.where(ids[None, :TOP_K] == expert, pad_start[:, None, None], 0), axis=0)
    dest = (start_of + rank[:TOP_K]).reshape(-1)
    n_blocks = (n * TOP_K + N_EXPERTS * (te - 1)) // te
    block_start = jnp.arange(n_blocks, dtype=jnp.int32) * te
    block_e = jnp.minimum(jnp.sum((pad_end[None, :] <= block_start[:, None]).astype(jnp.int32), axis=1),
                          N_EXPERTS - 1)
    n_used = (pad_end[-1:] // te).astype(jnp.int32)
    tail_start = jnp.where(padded > 0, pad_end - te, -1).astype(jnp.int32)
    eid = jnp.arange(N_EXPERTS, dtype=jnp.int32)
    later_used = (padded[None, :] > 0) & (eid[None, :] > eid[:, None])
    next_e = jnp.min(jnp.where(later_used, eid[None, :], N_EXPERTS), axis=1)
    next_e = jnp.where(next_e < N_EXPERTS, next_e, -1).astype(jnp.int32)

    xs = _dispatch_call(dest, tail_start, n_used, m, n_blocks * te, TD_DISPATCH, te)
    ys = _ffn_call(block_e, n_used, next_e, xs, w_gate[0], b_gate[0], w_up[0], b_up[0], w_down[0], b_down[0], te)
    out = _combine_call(dest, h1, gate, ys, TC_COMBINE)
    return out[None]
```
